```python
import math
import jax, jax.numpy as jnp
from jax import lax
import numpy as np

D_MODEL = 4096
BATCH = 1
SEQ = 8192
DEPTH = 1

D_MIX = D_MODEL
D_ATTN = D_MIX // 2
D_CONV = D_MIX - D_ATTN
N_HEADS = 16
QK_NOPE_DIM = 128
QK_ROPE_DIM = 64
QK_HEAD_DIM = QK_NOPE_DIM + QK_ROPE_DIM
V_HEAD_DIM = D_ATTN // N_HEADS
Q_LORA_RANK = 1024
KV_LORA_RANK = 512
ROPE_THETA = 10000.0
Q_BLOCK = 128
CONV_WIDTH = 31
N_EXPERTS = 32
TOP_K = 4
D_EXPERT = D_MODEL // 2
SWIGLU_LIMIT = 7.0
SWIGLU_ALPHA = 1.702
EXPERT_BLOCK = 128
EPS = 1e-6
D_IN = 2 * D_CONV + Q_LORA_RANK + KV_LORA_RANK + QK_ROPE_DIM

kernel_name = "hymba_conformer_mla_moe_encoder"


def rms_norm(x, w):
    xf = x.astype(jnp.float32)
    y = xf * lax.rsqrt(jnp.mean(xf * xf, axis=-1, keepdims=True) + EPS)
    return (y * w.astype(jnp.float32)).astype(x.dtype)


def layer_norm(x, w, b):
    xf = x.astype(jnp.float32)
    mu = jnp.mean(xf, axis=-1, keepdims=True)
    var = jnp.mean(jnp.square(xf - mu), axis=-1, keepdims=True)
    y = (xf - mu) * lax.rsqrt(var + EPS)
    return (y * w.astype(jnp.float32) + b.astype(jnp.float32)).astype(x.dtype)


def rope_tables(positions):
    inv_freq = ROPE_THETA ** (-jnp.arange(0, QK_ROPE_DIM, 2, dtype=jnp.float32) / QK_ROPE_DIM)
    ang = positions.astype(jnp.float32)[..., None] * inv_freq
    return jnp.cos(ang), jnp.sin(ang)


def apply_rope(t, cos, sin):
    tf = t.astype(jnp.float32)
    t1, t2 = tf[..., : QK_ROPE_DIM // 2], tf[..., QK_ROPE_DIM // 2:]
    out = jnp.concatenate([t1 * cos - t2 * sin, t2 * cos + t1 * sin], axis=-1)
    return out.astype(t.dtype)


def mla_group(z_q, z_kv, cos, sin, q_a_norm_w, w_uq, kv_a_norm_w, w_ukv, q_head_norm_w, k_head_norm_w):
    B, S, _ = z_q.shape
    q = (rms_norm(z_q, q_a_norm_w) @ w_uq).reshape(B, S, N_HEADS, QK_HEAD_DIM)
    q_nope, q_pe = q[..., :QK_NOPE_DIM], q[..., QK_NOPE_DIM:]
    q_pe = apply_rope(q_pe, cos[:, :, None, :], sin[:, :, None, :])
    c_kv, k_pe = z_kv[..., :KV_LORA_RANK], z_kv[..., KV_LORA_RANK:]
    kv = (rms_norm(c_kv, kv_a_norm_w) @ w_ukv).reshape(B, S, N_HEADS, QK_NOPE_DIM + V_HEAD_DIM)
    k_nope, v = kv[..., :QK_NOPE_DIM], kv[..., QK_NOPE_DIM:]
    k_pe = apply_rope(k_pe, cos, sin)
    k_pe = jnp.broadcast_to(k_pe[:, :, None, :], (B, S, N_HEADS, QK_ROPE_DIM))
    q = rms_norm(jnp.concatenate([q_nope, q_pe], axis=-1), q_head_norm_w)
    k = rms_norm(jnp.concatenate([k_nope, k_pe], axis=-1), k_head_norm_w)
    scale = QK_HEAD_DIM ** -0.5
    n_blocks = S // Q_BLOCK
    q_blocks = q.reshape(B, n_blocks, Q_BLOCK, N_HEADS, QK_HEAD_DIM).transpose(1, 0, 2, 3, 4)

    def attend(qb):
        s = jnp.einsum('bqhd,bkhd->bhqk', qb, k).astype(jnp.float32) * scale
        p = jax.nn.softmax(s, axis=-1).astype(v.dtype)
        return jnp.einsum('bhqk,bkhd->bqhd', p, v)

    o = lax.map(attend, q_blocks)
    return o.transpose(1, 0, 2, 3, 4).reshape(B, S, N_HEADS * V_HEAD_DIM)


def conformer_conv_group(z_glu, dw_w, dw_b, ln_w, ln_b):
    a, g = z_glu[..., :D_CONV], z_glu[..., D_CONV:]
    u = a * jax.nn.sigmoid(g)
    pad = CONV_WIDTH // 2
    u = lax.conv_general_dilated(
        u, dw_w, window_strides=(1,), padding=((pad, pad),),
        dimension_numbers=('NWC', 'WIO', 'NWC'), feature_group_count=D_CONV) + dw_b
    u = layer_norm(u, ln_w, ln_b)
    return jax.nn.silu(u)


def moe_ffn(h2d, w_router, b_router, w_gate, b_gate, w_up, b_up, w_down, b_down):
    T, D = h2d.shape
    logits = (h2d @ w_router + b_router).astype(jnp.float32)
    top_vals, top_idx = lax.top_k(logits, TOP_K)
    gates = jax.nn.softmax(top_vals, axis=-1)
    n_assign = T * TOP_K
    flat_e = top_idx.reshape(-1).astype(jnp.int32)
    flat_tok = jnp.repeat(jnp.arange(T, dtype=jnp.int32), TOP_K)
    flat_w = gates.reshape(-1)
    order = jnp.argsort(flat_e, stable=True)
    sorted_e, sorted_tok, sorted_w = flat_e[order], flat_tok[order], flat_w[order]
    counts = jnp.bincount(flat_e, length=N_EXPERTS).astype(jnp.int32)
    padded = ((counts + EXPERT_BLOCK - 1) // EXPERT_BLOCK) * EXPERT_BLOCK
    pad_end = jnp.cumsum(padded)
    pad_start = pad_end - padded
    start = jnp.cumsum(counts) - counts
    rank = jnp.arange(n_assign, dtype=jnp.int32) - start[sorted_e]
    dest = pad_start[sorted_e] + rank
    n_rows = n_assign + N_EXPERTS * EXPERT_BLOCK
    n_blocks = n_rows // EXPERT_BLOCK
    row_tok = jnp.zeros((n_rows,), jnp.int32).at[dest].set(sorted_tok)
    row_w = jnp.zeros((n_rows,), jnp.float32).at[dest].set(sorted_w)
    blk_start = jnp.arange(n_blocks, dtype=jnp.int32) * EXPERT_BLOCK
    blk_e = jnp.clip(jnp.searchsorted(pad_end, blk_start, side='right'), 0, N_EXPERTS - 1)

    def expert_block(args):
        e, toks = args
        xb = h2d[toks]
        gate = xb @ w_gate[e] + b_gate[e]
        lin = xb @ w_up[e] + b_up[e]
        gate = jnp.minimum(gate, SWIGLU_LIMIT)
        lin = jnp.clip(lin, -SWIGLU_LIMIT, SWIGLU_LIMIT)
        act = gate * jax.nn.sigmoid(SWIGLU_ALPHA * gate) * (lin + 1.0)
        return act @ w_down[e] + b_down[e]

    y_rows = lax.map(expert_block, (blk_e, row_tok.reshape(n_blocks, EXPERT_BLOCK)))
    y_rows = y_rows.reshape(n_rows, D) * row_w[:, None].astype(h2d.dtype)
    return jax.ops.segment_sum(y_rows, row_tok, num_segments=T)


def setup_inputs(seed: int = 0) -> dict:
    key = jax.random.key(seed)
    ks = jax.random.split(key, 32)

    def nrm(k, shape, scale):
        return jax.random.normal(k, shape, jnp.float32) * scale

    def gain(k, shape):
        return 1.0 + 0.01 * jax.random.normal(k, shape, jnp.float32)

    L = DEPTH
    x = jax.random.normal(ks[0], (BATCH, SEQ, D_MODEL), jnp.float32)
    offs = jax.random.randint(ks[1], (BATCH, 1), 0, 1024, dtype=jnp.int32)
    positions = jnp.arange(SEQ, dtype=jnp.int32)[None, :] + offs
    return {
        "x": x,
        "positions": positions,
        "attn_norm_w": gain(ks[2], (L, D_MODEL)),
        "w_in": nrm(ks[3], (L, D_MODEL, D_IN), D_MODEL ** -0.5),
        "q_a_norm_w": gain(ks[4], (L, Q_LORA_RANK)),
        "w_uq": nrm(ks[5], (L, Q_LORA_RANK, N_HEADS * QK_HEAD_DIM), Q_LORA_RANK ** -0.5),
        "kv_a_norm_w": gain(ks[6], (L, KV_LORA_RANK)),
        "w_ukv": nrm(ks[7], (L, KV_LORA_RANK, N_HEADS * (QK_NOPE_DIM + V_HEAD_DIM)), KV_LORA_RANK ** -0.5),
        "q_head_norm_w": gain(ks[8], (L, QK_HEAD_DIM)),
        "k_head_norm_w": gain(ks[9], (L, QK_HEAD_DIM)),
        "conv_dw_w": nrm(ks[10], (L, CONV_WIDTH, 1, D_CONV), CONV_WIDTH ** -0.5),
        "conv_dw_b": nrm(ks[11], (L, D_CONV), 0.01),
        "conv_ln_w": gain(ks[12], (L, D_CONV)),
        "conv_ln_b": nrm(ks[13], (L, D_CONV), 0.01),
        "attn_out_norm_w": gain(ks[14], (L, D_ATTN)),
        "conv_out_norm_w": gain(ks[15], (L, D_CONV)),
        "w_o": nrm(ks[16], (L, D_MIX, D_MODEL), D_MIX ** -0.5),
        "ffn_norm_w": gain(ks[17], (L, D_MODEL)),
        "w_router": nrm(ks[18], (L, D_MODEL, N_EXPERTS), D_MODEL ** -0.5),
        "b_router": nrm(ks[19], (L, N_EXPERTS), 0.01),
        "w_gate": nrm(ks[20], (L, N_EXPERTS, D_MODEL, D_EXPERT), D_MODEL ** -0.5),
        "b_gate": nrm(ks[21], (L, N_EXPERTS, D_EXPERT), 0.01),
        "w_up": nrm(ks[22], (L, N_EXPERTS, D_MODEL, D_EXPERT), D_MODEL ** -0.5),
        "b_up": nrm(ks[23], (L, N_EXPERTS, D_EXPERT), 0.01),
        "w_down": nrm(ks[24], (L, N_EXPERTS, D_EXPERT, D_MODEL), D_EXPERT ** -0.5),
        "b_down": nrm(ks[25], (L, N_EXPERTS, D_MODEL), 0.01),
    }


def reference(x, positions, attn_norm_w, w_in, q_a_norm_w, w_uq, kv_a_norm_w, w_ukv,
              q_head_norm_w, k_head_norm_w, conv_dw_w, conv_dw_b, conv_ln_w, conv_ln_b,
              attn_out_norm_w, conv_out_norm_w, w_o, ffn_norm_w, w_router, b_router,
              w_gate, b_gate, w_up, b_up, w_down, b_down):
    B, S, D = x.shape
    cos, sin = rope_tables(positions)
    for l in range(DEPTH):
        h = rms_norm(x, attn_norm_w[l])
        z = h @ w_in[l]
        z_glu = z[..., : 2 * D_CONV]
        z_q = z[..., 2 * D_CONV: 2 * D_CONV + Q_LORA_RANK]
        z_kv = z[..., 2 * D_CONV + Q_LORA_RANK:]
        attn = mla_group(z_q, z_kv, cos, sin, q_a_norm_w[l], w_uq[l], kv_a_norm_w[l], w_ukv[l],
                         q_head_norm_w[l], k_head_norm_w[l])
        conv = conformer_conv_group(z_glu, conv_dw_w[l], conv_dw_b[l], conv_ln_w[l], conv_ln_b[l])
        mixed = jnp.concatenate([rms_norm(attn, attn_out_norm_w[l]),
                                 rms_norm(conv, conv_out_norm_w[l])], axis=-1)
        x = x + mixed @ w_o[l]
        h2 = rms_norm(x, ffn_norm_w[l]).reshape(B * S, D)
        y = moe_ffn(h2, w_router[l], b_router[l], w_gate[l], b_gate[l], w_up[l], b_up[l],
                    w_down[l], b_down[l])
        x = x + y.reshape(B, S, D).astype(x.dtype)
    return x
```

```python
import functools

import jax
import jax.numpy as jnp
from jax import lax
from jax.experimental import pallas as pl
from jax.experimental.pallas import tpu as pltpu

F32, BF16, I32 = jnp.float32, jnp.bfloat16, jnp.int32

N_HEADS = 16
QK_NOPE_DIM = 128
QK_ROPE_DIM = 64
QK_HEAD_DIM = QK_NOPE_DIM + QK_ROPE_DIM
V_HEAD_DIM = 128
QK_PAD_DIM = 256
ROPE_THETA = 10000.0
CONV_WIDTH = 31
N_EXPERTS = 32
TOP_K = 4
SWIGLU_LIMIT = 7.0
SWIGLU_ALPHA = 1.702
EPS = 1e-6
LOG2E = 1.4426950408889634

LANES = 128
CONV_HALO = 16
VMEM_LIMIT_BYTES = 56 * 1024 * 1024


def _cparams(*sem):
    return pltpu.CompilerParams(dimension_semantics=sem, vmem_limit_bytes=VMEM_LIMIT_BYTES)


def _rms_scale(x, n):
    return lax.rsqrt(jnp.sum(x * x, axis=-1, keepdims=True) * (1.0 / n) + EPS)


def _norm_matmul_kernel(x_ref, nw_ref, w_ref, o_ref, h_ref):
    @pl.when(pl.program_id(1) == 0)
    def _():
        x = x_ref[...]
        h_ref[...] = (x * _rms_scale(x, x.shape[-1]) * nw_ref[...]).astype(BF16)

    o_ref[...] = jnp.dot(h_ref[...], w_ref[...], preferred_element_type=F32)


def _norm_matmul(x, nw, w, *, tm, tn):
    m, k = x.shape
    n = w.shape[1]
    return pl.pallas_call(
        _norm_matmul_kernel,
        grid=(m // tm, n // tn),
        in_specs=[pl.BlockSpec((tm, k), lambda i, j: (i, 0)),
                  pl.BlockSpec((1, k), lambda i, j: (0, 0)),
                  pl.BlockSpec((k, tn), lambda i, j: (0, j))],
        out_specs=pl.BlockSpec((tm, tn), lambda i, j: (i, j)),
        out_shape=jax.ShapeDtypeStruct((m, n), F32),
        scratch_shapes=[pltpu.VMEM((tm, k), BF16)],
        compiler_params=_cparams("parallel", "arbitrary"),
        name="in_proj",
    )(x, nw, w)


def _conv_kernel(a_ref, g_ref, ap_ref, gp_ref, an_ref, gn_ref, w_ref, b_ref, lnw_ref, lnb_ref, onw_ref,
                 o_ref, uext_ref, conv_ref, *, ts, nc, rb):
    i = pl.program_id(0)
    has_prev = i > 0
    has_next = i < pl.num_programs(0) - 1
    halo = CONV_HALO

    def glu(a, g):
        return a * jax.nn.sigmoid(g)

    for c in range(nc):
        sl = slice(c * LANES, (c + 1) * LANES)
        uext_ref[c, 0:halo, :] = jnp.where(has_prev, glu(ap_ref[:, sl], gp_ref[:, sl]), 0.0)
        uext_ref[c, halo:halo + ts, :] = glu(a_ref[:, sl], g_ref[:, sl])
        uext_ref[c, halo + ts:halo + ts + halo, :] = jnp.where(has_next, glu(an_ref[:, sl], gn_ref[:, sl]), 0.0)

    off0 = halo - CONV_WIDTH // 2

    def chunk_body(c, carry):
        w = w_ref[c]
        wk = [jnp.broadcast_to(w[k:k + 1, :], (rb, LANES)) for k in range(CONV_WIDTH)]
        bias = jnp.broadcast_to(b_ref[c], (rb, LANES))
        for r in range(ts // rb):
            acc = bias
            for k in range(CONV_WIDTH):
                acc = acc + wk[k] * uext_ref[c, r * rb + off0 + k:r * rb + off0 + k + rb, :]
            conv_ref[c, r * rb:(r + 1) * rb, :] = acc
        return carry

    lax.fori_loop(0, nc, chunk_body, 0)

    n_ch = nc * LANES
    xc = conv_ref[...]
    mu = jnp.sum(jnp.sum(xc, axis=0), axis=-1, keepdims=True) * (1.0 / n_ch)
    d = xc - mu[None]
    var = jnp.sum(jnp.sum(d * d, axis=0), axis=-1, keepdims=True) * (1.0 / n_ch)
    y = d * lax.rsqrt(var + EPS)[None] * lnw_ref[...] + lnb_ref[...]
    y = y * jax.nn.sigmoid(y)
    ms = jnp.sum(jnp.sum(y * y, axis=0), axis=-1, keepdims=True) * (1.0 / n_ch)
    o = y * lax.rsqrt(ms + EPS)[None] * onw_ref[...]
    for c in range(nc):
        o_ref[:, c * LANES:(c + 1) * LANES] = o[c].astype(BF16)


def _conv_group(z, dw_w, dw_b, ln_w, ln_b, on_w, *, n_ch, ts):
    s = z.shape[0]
    nc = n_ch // LANES
    hb = ts // CONV_HALO
    n_hblk = s // CONV_HALO

    def chunked(v):
        return v.reshape(nc, 1, LANES)

    w = dw_w.reshape(CONV_WIDTH, nc, LANES).transpose(1, 0, 2)
    w = jnp.pad(w, ((0, 0), (0, 32 - CONV_WIDTH), (0, 0)))
    vec_spec = pl.BlockSpec((nc, 1, LANES), lambda i: (0, 0, 0))
    kern = functools.partial(_conv_kernel, ts=ts, nc=nc, rb=64)
    return pl.pallas_call(
        kern,
        grid=(s // ts,),
        in_specs=[pl.BlockSpec((ts, n_ch), lambda i: (i, 0)),
                  pl.BlockSpec((ts, n_ch), lambda i: (i, 1)),
                  pl.BlockSpec((CONV_HALO, n_ch), lambda i: (jnp.maximum(i * hb - 1, 0), 0)),
                  pl.BlockSpec((CONV_HALO, n_ch), lambda i: (jnp.maximum(i * hb - 1, 0), 1)),
                  pl.BlockSpec((CONV_HALO, n_ch), lambda i: (jnp.minimum((i + 1) * hb, n_hblk - 1), 0)),
                  pl.BlockSpec((CONV_HALO, n_ch), lambda i: (jnp.minimum((i + 1) * hb, n_hblk - 1), 1)),
                  pl.BlockSpec((nc, 32, LANES), lambda i: (0, 0, 0)),
                  vec_spec, vec_spec, vec_spec, vec_spec],
        out_specs=pl.BlockSpec((ts, n_ch), lambda i: (i, 0)),
        out_shape=jax.ShapeDtypeStruct((s, n_ch), BF16),
        scratch_shapes=[pltpu.VMEM((nc, ts + 2 * CONV_HALO, LANES), F32),
                        pltpu.VMEM((nc, ts, LANES), F32)],
        compiler_params=_cparams("parallel"),
        name="conv_group",
    )(z, z, z, z, z, z, w, chunked(dw_b), chunked(ln_w), chunked(ln_b), chunked(on_w))


def _q_kernel(z_ref, nw_ref, w_ref, cos_ref, sin_ref, hw_ref, q_ref, zn_ref, *, q_scale):
    @pl.when(pl.program_id(1) == 0)
    def _():
        z = z_ref[...]
        zn_ref[...] = (z * _rms_scale(z, z.shape[-1]) * nw_ref[...]).astype(BF16)

    r = jnp.dot(zn_ref[...], w_ref[0], preferred_element_type=F32)
    nope = r[:, :LANES]
    rope = r[:, LANES:2 * LANES] * cos_ref[...] + r[:, 2 * LANES:] * sin_ref[...]
    ss = jnp.sum(nope * nope, axis=-1, keepdims=True) + jnp.sum(rope * rope, axis=-1, keepdims=True)
    inv = lax.rsqrt(ss * (1.0 / QK_HEAD_DIM) + EPS) * q_scale
    hw = hw_ref[...]
    q_ref[0, :, :LANES] = (nope * inv * hw[:, :LANES]).astype(BF16)
    q_ref[0, :, LANES:] = (rope * inv * hw[:, LANES:]).astype(BF16)


def _q_proj(z, nw, w3, cos_t, sin_t, hw, *, col_blk, tm, q_scale):
    s = z.shape[0]
    rank = w3.shape[1]
    kern = functools.partial(_q_kernel, q_scale=q_scale)
    return pl.pallas_call(
        kern,
        grid=(s // tm, N_HEADS),
        in_specs=[pl.BlockSpec((tm, rank), lambda i, h: (i, col_blk)),
                  pl.BlockSpec((1, rank), lambda i, h: (0, 0)),
                  pl.BlockSpec((1, rank, 3 * LANES), lambda i, h: (h, 0, 0)),
                  pl.BlockSpec((tm, LANES), lambda i, h: (i, 0)),
                  pl.BlockSpec((tm, LANES), lambda i, h: (i, 0)),
                  pl.BlockSpec((1, QK_PAD_DIM), lambda i, h: (0, 0))],
        out_specs=pl.BlockSpec((1, tm, QK_PAD_DIM), lambda i, h: (h, i, 0)),
        out_shape=jax.ShapeDtypeStruct((N_HEADS, s, QK_PAD_DIM), BF16),
        scratch_shapes=[pltpu.VMEM((tm, rank), BF16)],
        compiler_params=_cparams("parallel", "arbitrary"),
        name="q_proj",
    )(z, nw, w3, cos_t, sin_t, hw)


def _kv_kernel(z_ref, kpe_ref, kpes_ref, nw_ref, w_ref, cos_ref, sin_ref, hw_ref, k_ref, v_ref,
               zn_ref, kr_ref, ss_ref):
    @pl.when(pl.program_id(1) == 0)
    def _():
        z = z_ref[...]
        zn_ref[...] = (z * _rms_scale(z, z.shape[-1]) * nw_ref[...]).astype(BF16)
        kr = kpe_ref[...] * cos_ref[...] + kpes_ref[...] * sin_ref[...]
        kr_ref[...] = kr
        ss_ref[...] = jnp.sum(kr * kr, axis=-1, keepdims=True)

    r = jnp.dot(zn_ref[...], w_ref[0], preferred_element_type=F32)
    kn = r[:, :LANES]
    ss = jnp.sum(kn * kn, axis=-1, keepdims=True) + ss_ref[...]
    inv = lax.rsqrt(ss * (1.0 / QK_HEAD_DIM) + EPS)
    hw = hw_ref[...]
    k_ref[0, :, :LANES] = (kn * inv * hw[:, :LANES]).astype(BF16)
    k_ref[0, :, LANES:] = (kr_ref[...] * inv * hw[:, LANES:]).astype(BF16)
    v_ref[0] = r[:, LANES:].astype(BF16)


def _kv_proj(z, nw, w3, cos_t, sin_t, hw, *, ckv_blk, kpe_blk, tm):
    s = z.shape[0]
    rank = w3.shape[1]
    return pl.pallas_call(
        _kv_kernel,
        grid=(s // tm, N_HEADS),
        in_specs=[pl.BlockSpec((tm, rank), lambda i, h: (i, ckv_blk)),
                  pl.BlockSpec((tm, LANES), lambda i, h: (i, kpe_blk)),
                  pl.BlockSpec((tm, LANES), lambda i, h: (i, kpe_blk + 1)),
                  pl.BlockSpec((1, rank), lambda i, h: (0, 0)),
                  pl.BlockSpec((1, rank, 2 * LANES), lambda i, h: (h, 0, 0)),
                  pl.BlockSpec((tm, LANES), lambda i, h: (i, 0)),
                  pl.BlockSpec((tm, LANES), lambda i, h: (i, 0)),
                  pl.BlockSpec((1, QK_PAD_DIM), lambda i, h: (0, 0))],
        out_specs=[pl.BlockSpec((1, tm, QK_PAD_DIM), lambda i, h: (h, i, 0)),
                   pl.BlockSpec((1, tm, V_HEAD_DIM), lambda i, h: (h, i, 0))],
        out_shape=[jax.ShapeDtypeStruct((N_HEADS, s, QK_PAD_DIM), BF16),
                   jax.ShapeDtypeStruct((N_HEADS, s, V_HEAD_DIM), BF16)],
        scratch_shapes=[pltpu.VMEM((tm, rank), BF16),
                        pltpu.VMEM((tm, LANES), F32),
                        pltpu.VMEM((tm, 1), F32)],
        compiler_params=_cparams("parallel", "arbitrary"),
        name="kv_proj",
    )(z, z, z, nw, w3, cos_t, sin_t, hw)


def _attn_kernel(q_ref, k_ref, v_ref, o_ref, *, tk):
    q = q_ref[0]
    tq = q.shape[0]
    nk = k_ref.shape[1] // tk

    def body(j, carry):
        m, l, acc = carry
        start = pl.multiple_of(j * tk, tk)
        ks = k_ref[0, pl.ds(start, tk), :]
        vs = v_ref[0, pl.ds(start, tk), :]
        s = lax.dot_general(q, ks, (((1,), (1,)), ((), ())), preferred_element_type=F32)
        m_new = jnp.maximum(m, jnp.max(s, axis=-1, keepdims=True))
        p = jnp.exp2(s - m_new)
        alpha = jnp.exp2(m - m_new)
        l = alpha * l + jnp.sum(p, axis=-1, keepdims=True)
        acc = alpha * acc + jnp.dot(p.astype(BF16), vs, preferred_element_type=F32)
        return m_new, l, acc

    m0 = jnp.full((tq, 1), -jnp.inf, F32)
    l0 = jnp.zeros((tq, 1), F32)
    acc0 = jnp.zeros((tq, V_HEAD_DIM), F32)
    _, l, acc = lax.fori_loop(0, nk, body, (m0, l0, acc0))
    o_ref[...] = acc / l


def _attention(q, k, v, *, tq, tk):
    h, s, _ = q.shape
    kern = functools.partial(_attn_kernel, tk=tk)
    return pl.pallas_call(
        kern,
        grid=(h, s // tq),
        in_specs=[pl.BlockSpec((1, tq, QK_PAD_DIM), lambda hh, i: (hh, i, 0)),
                  pl.BlockSpec((1, s, QK_PAD_DIM), lambda hh, i: (hh, 0, 0)),
                  pl.BlockSpec((1, s, V_HEAD_DIM), lambda hh, i: (hh, 0, 0))],
        out_specs=pl.BlockSpec((tq, V_HEAD_DIM), lambda hh, i: (i, hh)),
        out_shape=jax.ShapeDtypeStruct((s, h * V_HEAD_DIM), F32),
        compiler_params=_cparams("parallel", "arbitrary"),
        name="attention",
    )(q, k, v)


def _out_proj_kernel(a_ref, c_ref, x_ref, nw_ref, wa_ref, wc_ref, o_ref, an_ref):
    @pl.when(pl.program_id(1) == 0)
    def _():
        a = a_ref[...]
        an_ref[...] = (a * _rms_scale(a, a.shape[-1]) * nw_ref[...]).astype(BF16)

    acc = jnp.dot(an_ref[...], wa_ref[...], preferred_element_type=F32)
    acc = acc + jnp.dot(c_ref[...], wc_ref[...], preferred_element_type=F32)
    o_ref[...] = x_ref[...] + acc


def _out_proj(attn, conv_n, x, nw, w_o, *, tm, tn):
    s, da = attn.shape
    dc = conv_n.shape[1]
    d = x.shape[1]
    assert da == dc, "the two head groups share one W_o row-block size"
    return pl.pallas_call(
        _out_proj_kernel,
        grid=(s // tm, d // tn),
        in_specs=[pl.BlockSpec((tm, da), lambda i, j: (i, 0)),
                  pl.BlockSpec((tm, dc), lambda i, j: (i, 0)),
                  pl.BlockSpec((tm, tn), lambda i, j: (i, j)),
                  pl.BlockSpec((1, da), lambda i, j: (0, 0)),
                  pl.BlockSpec((da, tn), lambda i, j: (0, j)),
                  pl.BlockSpec((dc, tn), lambda i, j: (1, j))],
        out_specs=pl.BlockSpec((tm, tn), lambda i, j: (i, j)),
        out_shape=jax.ShapeDtypeStruct((s, d), F32),
        scratch_shapes=[pltpu.VMEM((tm, da), BF16)],
        compiler_params=_cparams("parallel", "arbitrary"),
        name="out_proj",
    )(attn, conv_n, x, nw, w_o, w_o)


def _router_kernel(x_ref, nw_ref, wr_ref, br_ref, tri_ref, h_ref, idx_ref, gate_ref, rank_ref, cnt_ref, run_ref):
    @pl.when(pl.program_id(0) == 0)
    def _():
        run_ref[...] = jnp.zeros_like(run_ref)

    x = x_ref[...]
    h = x * _rms_scale(x, x.shape[-1]) * nw_ref[...]
    h_ref[...] = h
    tm = x.shape[0]
    logits = lax.dot_general(wr_ref[...], h, (((1,), (1,)), ((), ())), preferred_element_type=F32,
                             precision=lax.Precision.HIGHEST) + br_ref[...]
    eio = lax.broadcasted_iota(I32, (N_EXPERTS, tm), 0)
    vals, idxs = [], []
    cur = logits
    for _ in range(TOP_K):
        m = jnp.max(cur, axis=0, keepdims=True)
        idx = jnp.min(jnp.where(cur == m, eio, N_EXPERTS), axis=0, keepdims=True)
        vals.append(m)
        idxs.append(idx)
        cur = jnp.where(eio == idx, -jnp.inf, cur)
    es = [jnp.exp(v - vals[0]) for v in vals]
    den = es[0] + es[1] + es[2] + es[3]
    base = run_ref[...]
    for k in range(TOP_K):
        gate_ref[k:k + 1, :] = es[k] / den
        idx_ref[k:k + 1, :] = idxs[k]
        onehot = eio == idxs[k]
        before = jnp.dot(onehot.astype(BF16), tri_ref[...], preferred_element_type=F32)
        rank = jnp.sum(jnp.where(onehot, before + base, 0.0), axis=0, keepdims=True)
        rank_ref[k:k + 1, :] = rank.astype(I32)
        base = base + jnp.sum(onehot.astype(F32), axis=1, keepdims=True)
    run_ref[...] = base
    cnt_ref[...] = base.astype(I32)


def _router(x1, nw, wr_t, br, *, tm):
    t, d = x1.shape
    tri = (lax.broadcasted_iota(I32, (tm, tm), 0) < lax.broadcasted_iota(I32, (tm, tm), 1)).astype(BF16)
    row4 = pl.BlockSpec((TOP_K, tm), lambda i: (0, i))
    return pl.pallas_call(
        _router_kernel,
        grid=(t // tm,),
        in_specs=[pl.BlockSpec((tm, d), lambda i: (i, 0)),
                  pl.BlockSpec((1, d), lambda i: (0, 0)),
                  pl.BlockSpec((N_EXPERTS, d), lambda i: (0, 0)),
                  pl.BlockSpec((N_EXPERTS, 1), lambda i: (0, 0)),
                  pl.BlockSpec((tm, tm), lambda i: (0, 0))],
        out_specs=[pl.BlockSpec((tm, d), lambda i: (i, 0)), row4, row4, row4,
                   pl.BlockSpec((N_EXPERTS, 1), lambda i: (0, 0))],
        out_shape=[jax.ShapeDtypeStruct((t, d), F32),
                   jax.ShapeDtypeStruct((TOP_K, t), I32),
                   jax.ShapeDtypeStruct((TOP_K, t), F32),
                   jax.ShapeDtypeStruct((TOP_K, t), I32),
                   jax.ShapeDtypeStruct((N_EXPERTS, 1), I32)],
        scratch_shapes=[pltpu.VMEM((N_EXPERTS, 1), F32)],
        compiler_params=_cparams("arbitrary"),
        name="router",
    )(x1, nw, wr_t, br, tri)


def _gather_kernel(dest_ref, h_hbm, xs_in_hbm, xs_hbm, sem, *, tt):
    del xs_in_hbm
    base = pl.program_id(0) * tt

    def row_copy(src_row, dst_row):
        return pltpu.make_async_copy(h_hbm.at[pl.ds(src_row, 1)], xs_hbm.at[pl.ds(dst_row, 1)], sem)

    def start_body(t, carry):
        for k in range(TOP_K):
            row_copy(base + t, dest_ref[0, k, t]).start()
        return carry

    def wait_body(t, carry):
        for k in range(TOP_K):
            row_copy(0, 0).wait()
        return carry

    lax.fori_loop(0, tt, start_body, 0)
    lax.fori_loop(0, tt, wait_body, 0)


def _moe_gather(h2, dest, n_rows, *, tt):
    t, d = h2.shape
    dest3 = dest.reshape(TOP_K, t // tt, tt).transpose(1, 0, 2)
    xs0 = jnp.zeros((n_rows, d), F32)
    kern = functools.partial(_gather_kernel, tt=tt)
    return pl.pallas_call(
        kern,
        grid=(t // tt,),
        in_specs=[pl.BlockSpec((1, TOP_K, tt), lambda i: (i, 0, 0), memory_space=pltpu.SMEM),
                  pl.BlockSpec(memory_space=pl.ANY),
                  pl.BlockSpec(memory_space=pl.ANY)],
        out_specs=pl.BlockSpec(memory_space=pl.ANY),
        out_shape=jax.ShapeDtypeStruct((n_rows, d), F32),
        scratch_shapes=[pltpu.SemaphoreType.DMA(())],
        input_output_aliases={2: 0},
        compiler_params=_cparams("arbitrary"),
        name="moe_gather",
    )(dest3, h2, xs0)


def _expert_up_kernel(te_ref, nv_ref, x_ref, wg_ref, bg_ref, wu_ref, bu_ref, o_ref):
    del te_ref

    @pl.when(pl.program_id(1) < nv_ref[0])
    def _():
        xb = x_ref[...].astype(BF16)
        gate = jnp.dot(xb, wg_ref[0], preferred_element_type=F32) + bg_ref[0]
        lin = jnp.dot(xb, wu_ref[0], preferred_element_type=F32) + bu_ref[0]
        gate = jnp.minimum(gate, SWIGLU_LIMIT)
        lin = jnp.clip(lin, -SWIGLU_LIMIT, SWIGLU_LIMIT)
        act = gate * jax.nn.sigmoid(SWIGLU_ALPHA * gate) * (lin + 1.0)
        o_ref[...] = act.astype(BF16)

    @pl.when(pl.program_id(1) >= nv_ref[0])
    def _():
        o_ref[...] = jnp.zeros_like(o_ref)


def _expert_up(tile_e, n_valid, xs, wg, bg, wu, bu, *, tm, tn):
    r, d = xs.shape
    de = wg.shape[2]

    def row_blk(j, i, te, nv):
        return (jnp.minimum(i, nv[0] - 1), 0)

    def w_blk(j, i, te, nv):
        return (te[i], 0, j)

    return pl.pallas_call(
        _expert_up_kernel,
        grid_spec=pltpu.PrefetchScalarGridSpec(
            num_scalar_prefetch=2,
            grid=(de // tn, r // tm),
            in_specs=[pl.BlockSpec((tm, d), row_blk),
                      pl.BlockSpec((1, d, tn), w_blk),
                      pl.BlockSpec((1, 1, tn), w_blk),
                      pl.BlockSpec((1, d, tn), w_blk),
                      pl.BlockSpec((1, 1, tn), w_blk)],
            out_specs=pl.BlockSpec((tm, tn), lambda j, i, te, nv: (i, j)),
        ),
        out_shape=jax.ShapeDtypeStruct((r, de), BF16),
        compiler_params=_cparams("arbitrary", "arbitrary"),
        name="expert_up",
    )(tile_e, n_valid, xs, wg, bg, wu, bu)


def _expert_down_kernel(te_ref, nv_ref, a_ref, w_ref, b_ref, o_ref):
    del te_ref

    @pl.when(pl.program_id(1) < nv_ref[0])
    def _():
        o_ref[...] = jnp.dot(a_ref[...], w_ref[0], preferred_element_type=F32) + b_ref[0]

    @pl.when(pl.program_id(1) >= nv_ref[0])
    def _():
        o_ref[...] = jnp.zeros_like(o_ref)


def _expert_down(tile_e, n_valid, act, wd, bd, *, tm, tn):
    r, de = act.shape
    d = wd.shape[2]

    def row_blk(j, i, te, nv):
        return (jnp.minimum(i, nv[0] - 1), 0)

    def w_blk(j, i, te, nv):
        return (te[i], 0, j)

    return pl.pallas_call(
        _expert_down_kernel,
        grid_spec=pltpu.PrefetchScalarGridSpec(
            num_scalar_prefetch=2,
            grid=(d // tn, r // tm),
            in_specs=[pl.BlockSpec((tm, de), row_blk),
                      pl.BlockSpec((1, de, tn), w_blk),
                      pl.BlockSpec((1, 1, tn), w_blk)],
            out_specs=pl.BlockSpec((tm, tn), lambda j, i, te, nv: (i, j)),
        ),
        out_shape=jax.ShapeDtypeStruct((r, d), F32),
        compiler_params=_cparams("arbitrary", "arbitrary"),
        name="expert_down",
    )(tile_e, n_valid, act, wd, bd)


def _combine_kernel(dest_ref, x_ref, g_ref, ys_hbm, o_ref, buf_ref, sem, *, tt):
    def row_copy(src_row, k, t):
        return pltpu.make_async_copy(ys_hbm.at[pl.ds(src_row, 1)], buf_ref.at[k, pl.ds(t, 1)], sem)

    def start_body(t, carry):
        for k in range(TOP_K):
            row_copy(dest_ref[0, k, t], k, t).start()
        return carry

    def wait_body(t, carry):
        for k in range(TOP_K):
            row_copy(0, k, t).wait()
        return carry

    lax.fori_loop(0, tt, start_body, 0)
    lax.fori_loop(0, tt, wait_body, 0)
    g = g_ref[...]
    acc = x_ref[...]
    for k in range(TOP_K):
        acc = acc + g[:, k:k + 1] * buf_ref[k]
    o_ref[...] = acc


def _moe_combine(dest, x1, gates_t, ys, *, tt):
    t, d = x1.shape
    dest3 = dest.reshape(TOP_K, t // tt, tt).transpose(1, 0, 2)
    kern = functools.partial(_combine_kernel, tt=tt)
    return pl.pallas_call(
        kern,
        grid=(t // tt,),
        in_specs=[pl.BlockSpec((1, TOP_K, tt), lambda i: (i, 0, 0), memory_space=pltpu.SMEM),
                  pl.BlockSpec((tt, d), lambda i: (i, 0)),
                  pl.BlockSpec((tt, TOP_K), lambda i: (i, 0)),
                  pl.BlockSpec(memory_space=pl.ANY)],
        out_specs=pl.BlockSpec((tt, d), lambda i: (i, 0)),
        out_shape=jax.ShapeDtypeStruct((t, d), F32),
        scratch_shapes=[pltpu.VMEM((TOP_K, tt, d), F32), pltpu.SemaphoreType.DMA(())],
        compiler_params=_cparams("arbitrary"),
        name="moe_combine",
    )(dest3, x1, gates_t, ys)


def _rope_tables(positions):
    half = QK_ROPE_DIM // 2
    inv_freq = ROPE_THETA ** (-jnp.arange(0, QK_ROPE_DIM, 2, dtype=F32) / QK_ROPE_DIM)
    ang = positions.astype(F32)[:, None] * inv_freq
    cos, sin = jnp.cos(ang), jnp.sin(ang)
    zeros = jnp.zeros((positions.shape[0], LANES - 2 * half), F32)
    return jnp.concatenate([cos, cos, zeros], axis=1), jnp.concatenate([-sin, sin, zeros], axis=1)


def _swap_halves(w):
    half = w.shape[-1] // 2
    return jnp.concatenate([w[..., half:], w[..., :half]], axis=-1)


def _layer(x, positions, attn_norm_w, w_in, q_a_norm_w, w_uq, kv_a_norm_w, w_ukv, q_head_norm_w, k_head_norm_w,
           conv_dw_w, conv_dw_b, conv_ln_w, conv_ln_b, attn_out_norm_w, conv_out_norm_w, w_o, ffn_norm_w,
           w_router, b_router, w_gate, b_gate, w_up, b_up, w_down, b_down):
    s, d = x.shape
    q_rank = q_a_norm_w.shape[0]
    kv_rank = kv_a_norm_w.shape[0]
    n_conv = conv_dw_b.shape[0]
    glu_w = 2 * n_conv
    de = w_gate.shape[2]

    pad64 = jnp.zeros((d, LANES - QK_ROPE_DIM), F32)
    w_kpe = w_in[:, glu_w + q_rank + kv_rank:]
    z_cols = glu_w + q_rank + kv_rank + 2 * LANES
    in_tn = 1024
    z_pad = -z_cols % in_tn
    w_in_p = jnp.concatenate(
        [w_in[:, :glu_w + q_rank + kv_rank], w_kpe, pad64, _swap_halves(w_kpe), pad64, jnp.zeros((d, z_pad), F32)],
        axis=1).astype(BF16)

    wq = w_uq.reshape(q_rank, N_HEADS, QK_HEAD_DIM)
    wq_pe = wq[:, :, QK_NOPE_DIM:]
    zq = jnp.zeros((q_rank, N_HEADS, LANES - QK_ROPE_DIM), F32)
    wq3 = jnp.concatenate([wq[:, :, :QK_NOPE_DIM], wq_pe, zq, _swap_halves(wq_pe), zq], axis=2)
    wq3 = wq3.transpose(1, 0, 2).astype(BF16)
    wkv3 = w_ukv.reshape(kv_rank, N_HEADS, QK_NOPE_DIM + V_HEAD_DIM).transpose(1, 0, 2).astype(BF16)
    zero_hw = jnp.zeros((QK_PAD_DIM - QK_HEAD_DIM,), F32)
    q_hw = jnp.concatenate([q_head_norm_w, zero_hw])[None]
    k_hw = jnp.concatenate([k_head_norm_w, zero_hw])[None]
    cos_t, sin_t = _rope_tables(positions)

    z = _norm_matmul(x, attn_norm_w[None], w_in_p, tm=512, tn=in_tn)
    conv_n = _conv_group(z, conv_dw_w, conv_dw_b, conv_ln_w, conv_ln_b, conv_out_norm_w, n_ch=n_conv, ts=256)
    q_scale = QK_HEAD_DIM ** -0.5 * LOG2E
    q = _q_proj(z, q_a_norm_w[None], wq3, cos_t, sin_t, q_hw, col_blk=glu_w // q_rank, tm=512, q_scale=q_scale)
    k, v = _kv_proj(z, kv_a_norm_w[None], wkv3, cos_t, sin_t, k_hw, ckv_blk=(glu_w + q_rank) // kv_rank,
                    kpe_blk=(glu_w + q_rank + kv_rank) // LANES, tm=512)
    attn = _attention(q, k, v, tq=512, tk=512)
    x1 = _out_proj(attn, conv_n, x, attn_out_norm_w[None], w_o.astype(BF16), tm=512, tn=1024)

    tm_e = 512
    h2, top_idx, gates, rank, counts = _router(x1, ffn_norm_w[None], w_router.T, b_router[:, None], tm=512)
    counts = counts[:, 0]
    padded = (counts + tm_e - 1) // tm_e * tm_e
    pad_end = jnp.cumsum(padded)
    pad_start = pad_end - padded
    dest = pad_start[top_idx] + rank
    n_rows = s * TOP_K + N_EXPERTS * tm_e
    n_tiles = n_rows // tm_e
    n_valid = pad_end[-1] // tm_e
    tile_ids = jnp.arange(n_tiles, dtype=I32)
    tile_e = jnp.clip(jnp.searchsorted(pad_end, tile_ids * tm_e, side='right'), 0, N_EXPERTS - 1).astype(I32)
    tile_e = jnp.where(tile_ids < n_valid, tile_e, tile_e[n_valid - 1])
    n_valid = n_valid.astype(I32)[None]

    xs = _moe_gather(h2, dest, n_rows, tt=512)
    act = _expert_up(tile_e, n_valid, xs, w_gate.astype(BF16), b_gate[:, None, :], w_up.astype(BF16),
                     b_up[:, None, :], tm=tm_e, tn=512)
    ys = _expert_down(tile_e, n_valid, act, w_down.astype(BF16), b_down[:, None, :], tm=tm_e, tn=2048)
    return _moe_combine(dest, x1, gates.T, ys, tt=128)


def kernel(x, positions, attn_norm_w, w_in, q_a_norm_w, w_uq, kv_a_norm_w, w_ukv, q_head_norm_w, k_head_norm_w,
           conv_dw_w, conv_dw_b, conv_ln_w, conv_ln_b, attn_out_norm_w, conv_out_norm_w, w_o, ffn_norm_w,
           w_router, b_router, w_gate, b_gate, w_up, b_up, w_down, b_down):
    b, s, d = x.shape
    depth = attn_norm_w.shape[0]
    assert b == 1, "one sequence per call"
    xs = x[0]
    for l in range(depth):
        xs = _layer(xs, positions[0], attn_norm_w[l], w_in[l], q_a_norm_w[l], w_uq[l], kv_a_norm_w[l], w_ukv[l],
                    q_head_norm_w[l], k_head_norm_w[l], conv_dw_w[l][:, 0, :], conv_dw_b[l], conv_ln_w[l],
                    conv_ln_b[l], attn_out_norm_w[l], conv_out_norm_w[l], w_o[l], ffn_norm_w[l], w_router[l],
                    b_router[l], w_gate[l], b_gate[l], w_up[l], b_up[l], w_down[l], b_down[l])
    return xs[None]
```

```python
import functools

import jax
import jax.numpy as jnp
from jax import lax
from jax.experimental import pallas as pl
from jax.experimental.pallas import tpu as pltpu

F32, BF16, I32, U32 = jnp.float32, jnp.bfloat16, jnp.int32, jnp.uint32

N_HEADS = 16
QK_NOPE_DIM = 128
QK_ROPE_DIM = 64
QK_HEAD_DIM = QK_NOPE_DIM + QK_ROPE_DIM
V_HEAD_DIM = 128
QK_PAD_DIM = 256
ROPE_THETA = 10000.0
CONV_WIDTH = 31
N_EXPERTS = 32
TOP_K = 4
SWIGLU_LIMIT = 7.0
SWIGLU_ALPHA = 1.702
EPS = 1e-6
LOG2E = 1.4426950408889634

LANES = 128
CONV_HALO = 16
VMEM_LIMIT_BYTES = 56 * 1024 * 1024


def _cparams(*sem):
    return pltpu.CompilerParams(dimension_semantics=sem, vmem_limit_bytes=VMEM_LIMIT_BYTES)


def _rms_scale(x, n):
    return lax.rsqrt(jnp.sum(x * x, axis=-1, keepdims=True) * (1.0 / n) + EPS)


def _norm_matmul_kernel(x_ref, nw_ref, w_ref, o_ref, h_ref):
    @pl.when(pl.program_id(1) == 0)
    def _():
        x = x_ref[...]
        h_ref[...] = (x * _rms_scale(x, x.shape[-1]) * nw_ref[...]).astype(BF16)

    o_ref[...] = jnp.dot(h_ref[...], w_ref[...], preferred_element_type=F32)


def _norm_matmul(x, nw, w, *, tm, tn):
    m, k = x.shape
    n = w.shape[1]
    return pl.pallas_call(
        _norm_matmul_kernel,
        grid=(m // tm, n // tn),
        in_specs=[pl.BlockSpec((tm, k), lambda i, j: (i, 0)),
                  pl.BlockSpec((1, k), lambda i, j: (0, 0)),
                  pl.BlockSpec((k, tn), lambda i, j: (0, j))],
        out_specs=pl.BlockSpec((tm, tn), lambda i, j: (i, j)),
        out_shape=jax.ShapeDtypeStruct((m, n), F32),
        scratch_shapes=[pltpu.VMEM((tm, k), BF16)],
        compiler_params=_cparams("parallel", "arbitrary"),
        name="in_proj",
    )(x, nw, w)


def _conv_kernel(a_ref, g_ref, ap_ref, gp_ref, an_ref, gn_ref, w_ref, b_ref, lnw_ref, lnb_ref, onw_ref,
                 o_ref, uext_ref, conv_ref, *, ts, nc, rb):
    i = pl.program_id(0)
    has_prev = i > 0
    has_next = i < pl.num_programs(0) - 1
    halo = CONV_HALO

    def glu(a, g):
        return a * jax.nn.sigmoid(g)

    for c in range(nc):
        sl = slice(c * LANES, (c + 1) * LANES)
        uext_ref[c, 0:halo, :] = jnp.where(has_prev, glu(ap_ref[:, sl], gp_ref[:, sl]), 0.0)
        uext_ref[c, halo:halo + ts, :] = glu(a_ref[:, sl], g_ref[:, sl])
        uext_ref[c, halo + ts:halo + ts + halo, :] = jnp.where(has_next, glu(an_ref[:, sl], gn_ref[:, sl]), 0.0)

    off0 = halo - CONV_WIDTH // 2

    def chunk_body(c, carry):
        w = w_ref[c]
        wk = [jnp.broadcast_to(w[k:k + 1, :], (rb, LANES)) for k in range(CONV_WIDTH)]
        bias = jnp.broadcast_to(b_ref[c], (rb, LANES))
        for r in range(ts // rb):
            acc = bias
            for k in range(CONV_WIDTH):
                acc = acc + wk[k] * uext_ref[c, r * rb + off0 + k:r * rb + off0 + k + rb, :]
            conv_ref[c, r * rb:(r + 1) * rb, :] = acc
        return carry

    lax.fori_loop(0, nc, chunk_body, 0)

    n_ch = nc * LANES
    xc = conv_ref[...]
    mu = jnp.sum(jnp.sum(xc, axis=0), axis=-1, keepdims=True) * (1.0 / n_ch)
    d = xc - mu[None]
    var = jnp.sum(jnp.sum(d * d, axis=0), axis=-1, keepdims=True) * (1.0 / n_ch)
    y = d * lax.rsqrt(var + EPS)[None] * lnw_ref[...] + lnb_ref[...]
    y = y * jax.nn.sigmoid(y)
    ms = jnp.sum(jnp.sum(y * y, axis=0), axis=-1, keepdims=True) * (1.0 / n_ch)
    o = y * lax.rsqrt(ms + EPS)[None] * onw_ref[...]
    for c in range(nc):
        o_ref[:, c * LANES:(c + 1) * LANES] = o[c].astype(BF16)


def _conv_group(z, dw_w, dw_b, ln_w, ln_b, on_w, *, n_ch, ts):
    s = z.shape[0]
    nc = n_ch // LANES
    hb = ts // CONV_HALO
    n_hblk = s // CONV_HALO

    def chunked(v):
        return v.reshape(nc, 1, LANES)

    w = dw_w.reshape(CONV_WIDTH, nc, LANES).transpose(1, 0, 2)
    w = jnp.pad(w, ((0, 0), (0, 32 - CONV_WIDTH), (0, 0)))
    vec_spec = pl.BlockSpec((nc, 1, LANES), lambda i: (0, 0, 0))
    kern = functools.partial(_conv_kernel, ts=ts, nc=nc, rb=64)
    return pl.pallas_call(
        kern,
        grid=(s // ts,),
        in_specs=[pl.BlockSpec((ts, n_ch), lambda i: (i, 0)),
                  pl.BlockSpec((ts, n_ch), lambda i: (i, 1)),
                  pl.BlockSpec((CONV_HALO, n_ch), lambda i: (jnp.maximum(i * hb - 1, 0), 0)),
                  pl.BlockSpec((CONV_HALO, n_ch), lambda i: (jnp.maximum(i * hb - 1, 0), 1)),
                  pl.BlockSpec((CONV_HALO, n_ch), lambda i: (jnp.minimum((i + 1) * hb, n_hblk - 1), 0)),
                  pl.BlockSpec((CONV_HALO, n_ch), lambda i: (jnp.minimum((i + 1) * hb, n_hblk - 1), 1)),
                  pl.BlockSpec((nc, 32, LANES), lambda i: (0, 0, 0)),
                  vec_spec, vec_spec, vec_spec, vec_spec],
        out_specs=pl.BlockSpec((ts, n_ch), lambda i: (i, 0)),
        out_shape=jax.ShapeDtypeStruct((s, n_ch), BF16),
        scratch_shapes=[pltpu.VMEM((nc, ts + 2 * CONV_HALO, LANES), F32),
                        pltpu.VMEM((nc, ts, LANES), F32)],
        compiler_params=_cparams("parallel"),
        name="conv_group",
    )(z, z, z, z, z, z, w, chunked(dw_b), chunked(ln_w), chunked(ln_b), chunked(on_w))


def _q_kernel(z_ref, nw_ref, w_ref, cos_ref, sin_ref, hw_ref, q_ref, zn_ref, *, q_scale):
    @pl.when(pl.program_id(1) == 0)
    def _():
        z = z_ref[...]
        zn_ref[...] = (z * _rms_scale(z, z.shape[-1]) * nw_ref[...]).astype(BF16)

    r = jnp.dot(zn_ref[...], w_ref[0], preferred_element_type=F32)
    nope = r[:, :LANES]
    rope = r[:, LANES:2 * LANES] * cos_ref[...] + r[:, 2 * LANES:] * sin_ref[...]
    ss = jnp.sum(nope * nope, axis=-1, keepdims=True) + jnp.sum(rope * rope, axis=-1, keepdims=True)
    inv = lax.rsqrt(ss * (1.0 / QK_HEAD_DIM) + EPS) * q_scale
    hw = hw_ref[...]
    q_ref[0, :LANES, :] = (nope * inv * hw[:, :LANES]).T.astype(BF16)
    q_ref[0, LANES:, :] = (rope * inv * hw[:, LANES:]).T.astype(BF16)


def _q_proj(z, nw, w3, cos_t, sin_t, hw, *, col_blk, tm, q_scale):
    s = z.shape[0]
    rank = w3.shape[1]
    kern = functools.partial(_q_kernel, q_scale=q_scale)
    return pl.pallas_call(
        kern,
        grid=(s // tm, N_HEADS),
        in_specs=[pl.BlockSpec((tm, rank), lambda i, h: (i, col_blk)),
                  pl.BlockSpec((1, rank), lambda i, h: (0, 0)),
                  pl.BlockSpec((1, rank, 3 * LANES), lambda i, h: (h, 0, 0)),
                  pl.BlockSpec((tm, LANES), lambda i, h: (i, 0)),
                  pl.BlockSpec((tm, LANES), lambda i, h: (i, 0)),
                  pl.BlockSpec((1, QK_PAD_DIM), lambda i, h: (0, 0))],
        out_specs=pl.BlockSpec((1, QK_PAD_DIM, tm), lambda i, h: (h, 0, i)),
        out_shape=jax.ShapeDtypeStruct((N_HEADS, QK_PAD_DIM, s), BF16),
        scratch_shapes=[pltpu.VMEM((tm, rank), BF16)],
        compiler_params=_cparams("parallel", "arbitrary"),
        name="q_proj",
    )(z, nw, w3, cos_t, sin_t, hw)


def _kv_kernel(z_ref, kpe_ref, kpes_ref, nw_ref, w_ref, cos_ref, sin_ref, hw_ref, k_ref, v_ref,
               zn_ref, kr_ref, ss_ref):
    @pl.when(pl.program_id(1) == 0)
    def _():
        z = z_ref[...]
        zn_ref[...] = (z * _rms_scale(z, z.shape[-1]) * nw_ref[...]).astype(BF16)
        kr = kpe_ref[...] * cos_ref[...] + kpes_ref[...] * sin_ref[...]
        kr_ref[...] = kr
        ss_ref[...] = jnp.sum(kr * kr, axis=-1, keepdims=True)

    r = jnp.dot(zn_ref[...], w_ref[0], preferred_element_type=F32)
    kn = r[:, :LANES]
    ss = jnp.sum(kn * kn, axis=-1, keepdims=True) + ss_ref[...]
    inv = lax.rsqrt(ss * (1.0 / QK_HEAD_DIM) + EPS)
    hw = hw_ref[...]
    k_ref[0, :, :LANES] = (kn * inv * hw[:, :LANES]).astype(BF16)
    k_ref[0, :, LANES:] = (kr_ref[...] * inv * hw[:, LANES:]).astype(BF16)
    v_ref[0] = r[:, LANES:].T.astype(BF16)


def _kv_proj(z, nw, w3, cos_t, sin_t, hw, *, ckv_blk, kpe_blk, tm):
    s = z.shape[0]
    rank = w3.shape[1]
    return pl.pallas_call(
        _kv_kernel,
        grid=(s // tm, N_HEADS),
        in_specs=[pl.BlockSpec((tm, rank), lambda i, h: (i, ckv_blk)),
                  pl.BlockSpec((tm, LANES), lambda i, h: (i, kpe_blk)),
                  pl.BlockSpec((tm, LANES), lambda i, h: (i, kpe_blk + 1)),
                  pl.BlockSpec((1, rank), lambda i, h: (0, 0)),
                  pl.BlockSpec((1, rank, 2 * LANES), lambda i, h: (h, 0, 0)),
                  pl.BlockSpec((tm, LANES), lambda i, h: (i, 0)),
                  pl.BlockSpec((tm, LANES), lambda i, h: (i, 0)),
                  pl.BlockSpec((1, QK_PAD_DIM), lambda i, h: (0, 0))],
        out_specs=[pl.BlockSpec((1, tm, QK_PAD_DIM), lambda i, h: (h, i, 0)),
                   pl.BlockSpec((1, V_HEAD_DIM, tm), lambda i, h: (h, 0, i))],
        out_shape=[jax.ShapeDtypeStruct((N_HEADS, s, QK_PAD_DIM), BF16),
                   jax.ShapeDtypeStruct((N_HEADS, V_HEAD_DIM, s), BF16)],
        scratch_shapes=[pltpu.VMEM((tm, rank), BF16),
                        pltpu.VMEM((tm, LANES), F32),
                        pltpu.VMEM((tm, 1), F32)],
        compiler_params=_cparams("parallel", "arbitrary"),
        name="kv_proj",
    )(z, z, z, nw, w3, cos_t, sin_t, hw)


def _attn_kernel(qt_ref, k_ref, vt_ref, o_ref, *, tk):
    qt = qt_ref[0]
    tq = qt.shape[1]
    nk = k_ref.shape[1] // tk

    def body(j, carry):
        m, l, acc = carry
        start = pl.multiple_of(j * tk, tk)
        ks = k_ref[0, pl.ds(start, tk), :]
        vts = vt_ref[0, :, pl.ds(start, tk)]
        st = jnp.dot(ks, qt, preferred_element_type=F32)
        m_new = jnp.maximum(m, jnp.max(st, axis=0, keepdims=True))
        p = jnp.exp2(st - m_new)
        alpha = jnp.exp2(m - m_new)
        l = alpha * l + jnp.sum(p, axis=0, keepdims=True)
        acc = alpha * acc + jnp.dot(vts, p.astype(BF16), preferred_element_type=F32)
        return m_new, l, acc

    m0 = jnp.full((1, tq), -jnp.inf, F32)
    l0 = jnp.zeros((1, tq), F32)
    acc0 = jnp.zeros((V_HEAD_DIM, tq), F32)
    _, l, acc = lax.fori_loop(0, nk, body, (m0, l0, acc0), unroll=4)
    o_ref[...] = (acc / l).T


def _attention(qt, k, vt, *, tq, tk):
    h, s, _ = k.shape
    kern = functools.partial(_attn_kernel, tk=tk)
    return pl.pallas_call(
        kern,
        grid=(h, s // tq),
        in_specs=[pl.BlockSpec((1, QK_PAD_DIM, tq), lambda hh, i: (hh, 0, i)),
                  pl.BlockSpec((1, s, QK_PAD_DIM), lambda hh, i: (hh, 0, 0)),
                  pl.BlockSpec((1, V_HEAD_DIM, s), lambda hh, i: (hh, 0, 0))],
        out_specs=pl.BlockSpec((tq, V_HEAD_DIM), lambda hh, i: (i, hh)),
        out_shape=jax.ShapeDtypeStruct((s, h * V_HEAD_DIM), F32),
        compiler_params=_cparams("parallel", "arbitrary"),
        name="attention",
    )(qt, k, vt)


def _out_proj_kernel(a_ref, c_ref, x_ref, nw_ref, wa_ref, wc_ref, o_ref, an_ref):
    @pl.when(pl.program_id(1) == 0)
    def _():
        a = a_ref[...]
        an_ref[...] = (a * _rms_scale(a, a.shape[-1]) * nw_ref[...]).astype(BF16)

    acc = jnp.dot(an_ref[...], wa_ref[...], preferred_element_type=F32)
    acc = acc + jnp.dot(c_ref[...], wc_ref[...], preferred_element_type=F32)
    o_ref[...] = x_ref[...] + acc


def _out_proj(attn, conv_n, x, nw, w_o, *, tm, tn):
    s, da = attn.shape
    dc = conv_n.shape[1]
    d = x.shape[1]
    assert da == dc, "the two head groups share one W_o row-block size"
    return pl.pallas_call(
        _out_proj_kernel,
        grid=(s // tm, d // tn),
        in_specs=[pl.BlockSpec((tm, da), lambda i, j: (i, 0)),
                  pl.BlockSpec((tm, dc), lambda i, j: (i, 0)),
                  pl.BlockSpec((tm, tn), lambda i, j: (i, j)),
                  pl.BlockSpec((1, da), lambda i, j: (0, 0)),
                  pl.BlockSpec((da, tn), lambda i, j: (0, j)),
                  pl.BlockSpec((dc, tn), lambda i, j: (1, j))],
        out_specs=pl.BlockSpec((tm, tn), lambda i, j: (i, j)),
        out_shape=jax.ShapeDtypeStruct((s, d), F32),
        scratch_shapes=[pltpu.VMEM((tm, da), BF16)],
        compiler_params=_cparams("parallel", "arbitrary"),
        name="out_proj",
    )(attn, conv_n, x, nw, w_o, w_o)


def _router_kernel(x_ref, nw_ref, wr_ref, br_ref, tri_ref, h_ref, idx_ref, gate_ref, rank_ref, cnt_ref, run_ref):
    @pl.when(pl.program_id(0) == 0)
    def _():
        run_ref[...] = jnp.zeros_like(run_ref)

    x = x_ref[...]
    h = x * _rms_scale(x, x.shape[-1]) * nw_ref[...]
    half = h.shape[1] // 2
    lo = lax.bitcast_convert_type(h[:, :half].astype(BF16).astype(F32), U32)
    hi = lax.bitcast_convert_type(h[:, half:].astype(BF16).astype(F32), U32)
    h_ref[...] = (lo >> 16) | hi
    tm = x.shape[0]
    logits = lax.dot_general(wr_ref[...], h, (((1,), (1,)), ((), ())), preferred_element_type=F32,
                             precision=lax.Precision.HIGHEST) + br_ref[...]
    eio = lax.broadcasted_iota(I32, (N_EXPERTS, tm), 0)
    vals, idxs = [], []
    cur = logits
    for _ in range(TOP_K):
        m = jnp.max(cur, axis=0, keepdims=True)
        idx = jnp.min(jnp.where(cur == m, eio, N_EXPERTS), axis=0, keepdims=True)
        vals.append(m)
        idxs.append(idx)
        cur = jnp.where(eio == idx, -jnp.inf, cur)
    es = [jnp.exp(v - vals[0]) for v in vals]
    den = es[0] + es[1] + es[2] + es[3]
    base = run_ref[...]
    for k in range(TOP_K):
        gate_ref[k:k + 1, :] = es[k] / den
        idx_ref[k:k + 1, :] = idxs[k]
        onehot = eio == idxs[k]
        before = jnp.dot(onehot.astype(BF16), tri_ref[...], preferred_element_type=F32)
        rank = jnp.sum(jnp.where(onehot, before + base, 0.0), axis=0, keepdims=True)
        rank_ref[k:k + 1, :] = rank.astype(I32)
        base = base + jnp.sum(onehot.astype(F32), axis=1, keepdims=True)
    run_ref[...] = base
    cnt_ref[...] = base.astype(I32)


def _router(x1, nw, wr_t, br, *, tm):
    t, d = x1.shape
    tri = (lax.broadcasted_iota(I32, (tm, tm), 0) < lax.broadcasted_iota(I32, (tm, tm), 1)).astype(BF16)
    row4 = pl.BlockSpec((TOP_K, tm), lambda i: (0, i))
    return pl.pallas_call(
        _router_kernel,
        grid=(t // tm,),
        in_specs=[pl.BlockSpec((tm, d), lambda i: (i, 0)),
                  pl.BlockSpec((1, d), lambda i: (0, 0)),
                  pl.BlockSpec((N_EXPERTS, d), lambda i: (0, 0)),
                  pl.BlockSpec((N_EXPERTS, 1), lambda i: (0, 0)),
                  pl.BlockSpec((tm, tm), lambda i: (0, 0))],
        out_specs=[pl.BlockSpec((tm, d // 2), lambda i: (i, 0)), row4, row4, row4,
                   pl.BlockSpec((N_EXPERTS, 1), lambda i: (0, 0))],
        out_shape=[jax.ShapeDtypeStruct((t, d // 2), U32),
                   jax.ShapeDtypeStruct((TOP_K, t), I32),
                   jax.ShapeDtypeStruct((TOP_K, t), F32),
                   jax.ShapeDtypeStruct((TOP_K, t), I32),
                   jax.ShapeDtypeStruct((N_EXPERTS, 1), I32)],
        scratch_shapes=[pltpu.VMEM((N_EXPERTS, 1), F32)],
        compiler_params=_cparams("arbitrary"),
        name="router",
    )(x1, nw, wr_t, br, tri)


def _gather_kernel(dest_ref, h_ref, xs_in_hbm, xs_hbm, sem, *, tt):
    del xs_in_hbm

    def row_copy(src_row, dst_row):
        return pltpu.make_async_copy(h_ref.at[pl.ds(src_row, 1)], xs_hbm.at[pl.ds(dst_row, 1)], sem)

    def start_body(t, carry):
        for k in range(TOP_K):
            row_copy(t, dest_ref[0, k, t]).start()
        return carry

    def wait_body(t, carry):
        for k in range(TOP_K):
            row_copy(0, 0).wait()
        return carry

    lax.fori_loop(0, tt, start_body, 0)
    lax.fori_loop(0, tt, wait_body, 0)


def _moe_gather(h2, dest, n_rows, *, tt):
    t, d = h2.shape
    dest3 = dest.reshape(TOP_K, t // tt, tt).transpose(1, 0, 2)
    xs0 = jnp.zeros((n_rows, d), h2.dtype)
    kern = functools.partial(_gather_kernel, tt=tt)
    return pl.pallas_call(
        kern,
        grid=(t // tt,),
        in_specs=[pl.BlockSpec((1, TOP_K, tt), lambda i: (i, 0, 0), memory_space=pltpu.SMEM),
                  pl.BlockSpec((tt, d), lambda i: (i, 0)),
                  pl.BlockSpec(memory_space=pl.ANY)],
        out_specs=pl.BlockSpec(memory_space=pl.ANY),
        out_shape=jax.ShapeDtypeStruct((n_rows, d), h2.dtype),
        scratch_shapes=[pltpu.SemaphoreType.DMA(())],
        input_output_aliases={2: 0},
        compiler_params=_cparams("arbitrary"),
        name="moe_gather",
    )(dest3, h2, xs0)


def _expert_changed(te_ref, i):
    return jnp.logical_or(i == 0, te_ref[i] != te_ref[jnp.maximum(i - 1, 0)])


def _expert_up_kernel(te_ref, nv_ref, x_ref, wg_ref, bg_ref, wu_ref, bu_ref, o_ref, wgc_ref, wuc_ref):
    i = pl.program_id(1)
    valid = i < nv_ref[0]

    @pl.when(jnp.logical_and(valid, _expert_changed(te_ref, i)))
    def _():
        wgc_ref[...] = wg_ref[0].astype(BF16)
        wuc_ref[...] = wu_ref[0].astype(BF16)

    @pl.when(valid)
    def _():
        xw = x_ref[...]
        half = xw.shape[1]
        x_lo = lax.bitcast_convert_type(xw << 16, F32).astype(BF16)
        x_hi = lax.bitcast_convert_type((xw >> 16) << 16, F32).astype(BF16)

        def proj(w_ref, b_ref):
            y = jnp.dot(x_lo, w_ref[:half, :], preferred_element_type=F32)
            return y + jnp.dot(x_hi, w_ref[half:, :], preferred_element_type=F32) + b_ref[0]

        gate = proj(wgc_ref, bg_ref)
        lin = proj(wuc_ref, bu_ref)
        gate = jnp.minimum(gate, SWIGLU_LIMIT)
        lin = jnp.clip(lin, -SWIGLU_LIMIT, SWIGLU_LIMIT)
        act = gate * jax.nn.sigmoid(SWIGLU_ALPHA * gate) * (lin + 1.0)
        o_ref[...] = act.astype(BF16)

    @pl.when(jnp.logical_not(valid))
    def _():
        o_ref[...] = jnp.zeros_like(o_ref)


def _expert_up(tile_e, n_valid, xs, wg, bg, wu, bu, *, tm, tn):
    r, dw = xs.shape
    d, de = wg.shape[1], wg.shape[2]
    assert 2 * dw == d, "xs rows hold two bf16 columns per 32-bit word"

    def row_blk(j, i, te, nv):
        return (jnp.minimum(i, nv[0] - 1), 0)

    def w_blk(j, i, te, nv):
        return (te[i], 0, j)

    return pl.pallas_call(
        _expert_up_kernel,
        grid_spec=pltpu.PrefetchScalarGridSpec(
            num_scalar_prefetch=2,
            grid=(de // tn, r // tm),
            in_specs=[pl.BlockSpec((tm, dw), row_blk),
                      pl.BlockSpec((1, d, tn), w_blk),
                      pl.BlockSpec((1, 1, tn), w_blk),
                      pl.BlockSpec((1, d, tn), w_blk),
                      pl.BlockSpec((1, 1, tn), w_blk)],
            out_specs=pl.BlockSpec((tm, tn), lambda j, i, te, nv: (i, j)),
            scratch_shapes=[pltpu.VMEM((d, tn), BF16), pltpu.VMEM((d, tn), BF16)],
        ),
        out_shape=jax.ShapeDtypeStruct((r, de), BF16),
        compiler_params=_cparams("arbitrary", "arbitrary"),
        name="expert_up",
    )(tile_e, n_valid, xs, wg, bg, wu, bu)


def _expert_down_kernel(te_ref, nv_ref, a_ref, w_ref, b_ref, o_ref, wc_ref):
    i = pl.program_id(1)
    valid = i < nv_ref[0]

    @pl.when(jnp.logical_and(valid, _expert_changed(te_ref, i)))
    def _():
        wc_ref[...] = w_ref[0].astype(BF16)

    @pl.when(valid)
    def _():
        o_ref[...] = jnp.dot(a_ref[...], wc_ref[...], preferred_element_type=F32) + b_ref[0]

    @pl.when(jnp.logical_not(valid))
    def _():
        o_ref[...] = jnp.zeros_like(o_ref)


def _expert_down(tile_e, n_valid, act, wd, bd, *, tm, tn):
    r, de = act.shape
    d = wd.shape[2]

    def row_blk(j, i, te, nv):
        return (jnp.minimum(i, nv[0] - 1), 0)

    def w_blk(j, i, te, nv):
        return (te[i], 0, j)

    return pl.pallas_call(
        _expert_down_kernel,
        grid_spec=pltpu.PrefetchScalarGridSpec(
            num_scalar_prefetch=2,
            grid=(d // tn, r // tm),
            in_specs=[pl.BlockSpec((tm, de), row_blk),
                      pl.BlockSpec((1, de, tn), w_blk),
                      pl.BlockSpec((1, 1, tn), w_blk)],
            out_specs=pl.BlockSpec((tm, tn), lambda j, i, te, nv: (i, j)),
            scratch_shapes=[pltpu.VMEM((de, tn), BF16)],
        ),
        out_shape=jax.ShapeDtypeStruct((r, d), F32),
        compiler_params=_cparams("arbitrary", "arbitrary"),
        name="expert_down",
    )(tile_e, n_valid, act, wd, bd)


def _combine_kernel(dest_ref, x_ref, g_ref, ys_hbm, o_ref, buf_ref, sem, *, tt):
    def row_copy(src_row, k, t):
        return pltpu.make_async_copy(ys_hbm.at[pl.ds(src_row, 1)], buf_ref.at[k, pl.ds(t, 1)], sem)

    def start_body(t, carry):
        for k in range(TOP_K):
            row_copy(dest_ref[0, k, t], k, t).start()
        return carry

    def wait_body(t, carry):
        for k in range(TOP_K):
            row_copy(0, k, t).wait()
        return carry

    lax.fori_loop(0, tt, start_body, 0)
    lax.fori_loop(0, tt, wait_body, 0)
    g = g_ref[...]
    acc = x_ref[...]
    for k in range(TOP_K):
        acc = acc + g[:, k:k + 1] * buf_ref[k]
    o_ref[...] = acc


def _moe_combine(dest, x1, gates_t, ys, *, tt):
    t, d = x1.shape
    dest3 = dest.reshape(TOP_K, t // tt, tt).transpose(1, 0, 2)
    kern = functools.partial(_combine_kernel, tt=tt)
    return pl.pallas_call(
        kern,
        grid=(t // tt,),
        in_specs=[pl.BlockSpec((1, TOP_K, tt), lambda i: (i, 0, 0), memory_space=pltpu.SMEM),
                  pl.BlockSpec((tt, d), lambda i: (i, 0)),
                  pl.BlockSpec((tt, TOP_K), lambda i: (i, 0)),
                  pl.BlockSpec(memory_space=pl.ANY)],
        out_specs=pl.BlockSpec((tt, d), lambda i: (i, 0)),
        out_shape=jax.ShapeDtypeStruct((t, d), F32),
        scratch_shapes=[pltpu.VMEM((TOP_K, tt, d), F32), pltpu.SemaphoreType.DMA(())],
        compiler_params=_cparams("arbitrary"),
        name="moe_combine",
    )(dest3, x1, gates_t, ys)


def _rope_tables(positions):
    half = QK_ROPE_DIM // 2
    inv_freq = ROPE_THETA ** (-jnp.arange(0, QK_ROPE_DIM, 2, dtype=F32) / QK_ROPE_DIM)
    ang = positions.astype(F32)[:, None] * inv_freq
    cos, sin = jnp.cos(ang), jnp.sin(ang)
    zeros = jnp.zeros((positions.shape[0], LANES - 2 * half), F32)
    return jnp.concatenate([cos, cos, zeros], axis=1), jnp.concatenate([-sin, sin, zeros], axis=1)


def _swap_halves(w):
    half = w.shape[-1] // 2
    return jnp.concatenate([w[..., half:], w[..., :half]], axis=-1)


def _layer(x, positions, attn_norm_w, w_in, q_a_norm_w, w_uq, kv_a_norm_w, w_ukv, q_head_norm_w, k_head_norm_w,
           conv_dw_w, conv_dw_b, conv_ln_w, conv_ln_b, attn_out_norm_w, conv_out_norm_w, w_o, ffn_norm_w,
           w_router, b_router, w_gate, b_gate, w_up, b_up, w_down, b_down):
    s, d = x.shape
    q_rank = q_a_norm_w.shape[0]
    kv_rank = kv_a_norm_w.shape[0]
    n_conv = conv_dw_b.shape[0]
    glu_w = 2 * n_conv
    de = w_gate.shape[2]

    pad64 = jnp.zeros((d, LANES - QK_ROPE_DIM), F32)
    w_kpe = w_in[:, glu_w + q_rank + kv_rank:]
    z_cols = glu_w + q_rank + kv_rank + 2 * LANES
    in_tn = 1024
    z_pad = -z_cols % in_tn
    w_in_p = jnp.concatenate(
        [w_in[:, :glu_w + q_rank + kv_rank], w_kpe, pad64, _swap_halves(w_kpe), pad64, jnp.zeros((d, z_pad), F32)],
        axis=1).astype(BF16)

    wq = w_uq.reshape(q_rank, N_HEADS, QK_HEAD_DIM)
    wq_pe = wq[:, :, QK_NOPE_DIM:]
    zq = jnp.zeros((q_rank, N_HEADS, LANES - QK_ROPE_DIM), F32)
    wq3 = jnp.concatenate([wq[:, :, :QK_NOPE_DIM], wq_pe, zq, _swap_halves(wq_pe), zq], axis=2)
    wq3 = wq3.transpose(1, 0, 2).astype(BF16)
    wkv3 = w_ukv.reshape(kv_rank, N_HEADS, QK_NOPE_DIM + V_HEAD_DIM).transpose(1, 0, 2).astype(BF16)
    zero_hw = jnp.zeros((QK_PAD_DIM - QK_HEAD_DIM,), F32)
    q_hw = jnp.concatenate([q_head_norm_w, zero_hw])[None]
    k_hw = jnp.concatenate([k_head_norm_w, zero_hw])[None]
    cos_t, sin_t = _rope_tables(positions)

    z = _norm_matmul(x, attn_norm_w[None], w_in_p, tm=512, tn=in_tn)
    conv_n = _conv_group(z, conv_dw_w, conv_dw_b, conv_ln_w, conv_ln_b, conv_out_norm_w, n_ch=n_conv, ts=256)
    q_scale = QK_HEAD_DIM ** -0.5 * LOG2E
    q = _q_proj(z, q_a_norm_w[None], wq3, cos_t, sin_t, q_hw, col_blk=glu_w // q_rank, tm=512, q_scale=q_scale)
    k, v = _kv_proj(z, kv_a_norm_w[None], wkv3, cos_t, sin_t, k_hw, ckv_blk=(glu_w + q_rank) // kv_rank,
                    kpe_blk=(glu_w + q_rank + kv_rank) // LANES, tm=512)
    attn = _attention(q, k, v, tq=512, tk=512)
    x1 = _out_proj(attn, conv_n, x, attn_out_norm_w[None], w_o.astype(BF16), tm=512, tn=1024)

    tm_e = 512
    h2, top_idx, gates, rank, counts = _router(x1, ffn_norm_w[None], w_router.T, b_router[:, None], tm=512)
    counts = counts[:, 0]
    padded = (counts + tm_e - 1) // tm_e * tm_e
    pad_end = jnp.cumsum(padded)
    pad_start = pad_end - padded
    expert_ids = jnp.arange(N_EXPERTS, dtype=I32)[:, None, None]
    dest = jnp.sum(jnp.where(top_idx[None] == expert_ids, pad_start[:, None, None], 0), axis=0) + rank
    n_rows = s * TOP_K + N_EXPERTS * tm_e
    n_tiles = n_rows // tm_e
    n_valid = pad_end[-1] // tm_e
    tile_ids = jnp.arange(n_tiles, dtype=I32)
    tile_e = jnp.clip(jnp.searchsorted(pad_end, tile_ids * tm_e, side='right'), 0, N_EXPERTS - 1).astype(I32)
    tile_e = jnp.where(tile_ids < n_valid, tile_e, tile_e[n_valid - 1])
    n_valid = n_valid.astype(I32)[None]

    xs = _moe_gather(h2, dest, n_rows, tt=512)
    act = _expert_up(tile_e, n_valid, xs, w_gate, b_gate[:, None, :], w_up, b_up[:, None, :], tm=tm_e, tn=512)
    ys = _expert_down(tile_e, n_valid, act, w_down, b_down[:, None, :], tm=tm_e, tn=1024)
    return _moe_combine(dest, x1, gates.T, ys, tt=128)


def kernel(x, positions, attn_norm_w, w_in, q_a_norm_w, w_uq, kv_a_norm_w, w_ukv, q_head_norm_w, k_head_norm_w,
           conv_dw_w, conv_dw_b, conv_ln_w, conv_ln_b, attn_out_norm_w, conv_out_norm_w, w_o, ffn_norm_w,
           w_router, b_router, w_gate, b_gate, w_up, b_up, w_down, b_down):
    b, s, d = x.shape
    depth = attn_norm_w.shape[0]
    assert b == 1, "one sequence per call"
    xs = x[0]
    for l in range(depth):
        xs = _layer(xs, positions[0], attn_norm_w[l], w_in[l], q_a_norm_w[l], w_uq[l], kv_a_norm_w[l], w_ukv[l],
                    q_head_norm_w[l], k_head_norm_w[l], conv_dw_w[l][:, 0, :], conv_dw_b[l], conv_ln_w[l],
                    conv_ln_b[l], attn_out_norm_w[l], conv_out_norm_w[l], w_o[l], ffn_norm_w[l], w_router[l],
                    b_router[l], w_gate[l], b_gate[l], w_up[l], b_up[l], w_down[l], b_down[l])
    return xs[None]
```

```python
import functools

import jax
import jax.numpy as jnp
from jax import lax
from jax.experimental import pallas as pl
from jax.experimental.pallas import tpu as pltpu

F32, BF16, I32, U32 = jnp.float32, jnp.bfloat16, jnp.int32, jnp.uint32

N_HEADS = 16
QK_NOPE_DIM = 128
QK_ROPE_DIM = 64
QK_HEAD_DIM = QK_NOPE_DIM + QK_ROPE_DIM
V_HEAD_DIM = 128
QK_PAD_DIM = 256
ROPE_THETA = 10000.0
CONV_WIDTH = 31
N_EXPERTS = 32
TOP_K = 4
SWIGLU_LIMIT = 7.0
SWIGLU_ALPHA = 1.702
EPS = 1e-6
LOG2E = 1.4426950408889634

LANES = 128
CONV_HALO = 16
VMEM_LIMIT_BYTES = 56 * 1024 * 1024


def _cparams(*sem):
    return pltpu.CompilerParams(dimension_semantics=sem, vmem_limit_bytes=VMEM_LIMIT_BYTES)


def _rms_scale(x, n):
    return lax.rsqrt(jnp.sum(x * x, axis=-1, keepdims=True) * (1.0 / n) + EPS)


def _norm_matmul_kernel(x_ref, nw_ref, w_ref, o_ref, h_ref):
    @pl.when(pl.program_id(1) == 0)
    def _():
        x = x_ref[...]
        h_ref[...] = (x * _rms_scale(x, x.shape[-1]) * nw_ref[...]).astype(BF16)

    o_ref[...] = jnp.dot(h_ref[...], w_ref[...], preferred_element_type=F32)


def _norm_matmul(x, nw, w, *, tm, tn):
    m, k = x.shape
    n = w.shape[1]
    return pl.pallas_call(
        _norm_matmul_kernel,
        grid=(m // tm, n // tn),
        in_specs=[pl.BlockSpec((tm, k), lambda i, j: (i, 0)),
                  pl.BlockSpec((1, k), lambda i, j: (0, 0)),
                  pl.BlockSpec((k, tn), lambda i, j: (0, j))],
        out_specs=pl.BlockSpec((tm, tn), lambda i, j: (i, j)),
        out_shape=jax.ShapeDtypeStruct((m, n), F32),
        scratch_shapes=[pltpu.VMEM((tm, k), BF16)],
        compiler_params=_cparams("parallel", "arbitrary"),
        name="in_proj",
    )(x, nw, w)


def _conv_kernel(a_ref, g_ref, ap_ref, gp_ref, an_ref, gn_ref, w_ref, b_ref, lnw_ref, lnb_ref, onw_ref,
                 o_ref, uext_ref, conv_ref, *, ts, nc, rb):
    i = pl.program_id(0)
    has_prev = i > 0
    has_next = i < pl.num_programs(0) - 1
    halo = CONV_HALO

    def glu(a, g):
        return a * jax.nn.sigmoid(g)

    for c in range(nc):
        sl = slice(c * LANES, (c + 1) * LANES)
        uext_ref[c, 0:halo, :] = jnp.where(has_prev, glu(ap_ref[:, sl], gp_ref[:, sl]), 0.0)
        uext_ref[c, halo:halo + ts, :] = glu(a_ref[:, sl], g_ref[:, sl])
        uext_ref[c, halo + ts:halo + ts + halo, :] = jnp.where(has_next, glu(an_ref[:, sl], gn_ref[:, sl]), 0.0)

    off0 = halo - CONV_WIDTH // 2

    def chunk_body(c, carry):
        w = w_ref[c]
        wk = [jnp.broadcast_to(w[k:k + 1, :], (rb, LANES)) for k in range(CONV_WIDTH)]
        bias = jnp.broadcast_to(b_ref[c], (rb, LANES))
        for r in range(ts // rb):
            acc = bias
            for k in range(CONV_WIDTH):
                acc = acc + wk[k] * uext_ref[c, r * rb + off0 + k:r * rb + off0 + k + rb, :]
            conv_ref[c, r * rb:(r + 1) * rb, :] = acc
        return carry

    lax.fori_loop(0, nc, chunk_body, 0)

    n_ch = nc * LANES
    xc = conv_ref[...]
    mu = jnp.sum(jnp.sum(xc, axis=0), axis=-1, keepdims=True) * (1.0 / n_ch)
    d = xc - mu[None]
    var = jnp.sum(jnp.sum(d * d, axis=0), axis=-1, keepdims=True) * (1.0 / n_ch)
    y = d * lax.rsqrt(var + EPS)[None] * lnw_ref[...] + lnb_ref[...]
    y = y * jax.nn.sigmoid(y)
    ms = jnp.sum(jnp.sum(y * y, axis=0), axis=-1, keepdims=True) * (1.0 / n_ch)
    o = y * lax.rsqrt(ms + EPS)[None] * onw_ref[...]
    for c in range(nc):
        o_ref[:, c * LANES:(c + 1) * LANES] = o[c].astype(BF16)


def _conv_group(z, dw_w, dw_b, ln_w, ln_b, on_w, *, n_ch, ts):
    s = z.shape[0]
    nc = n_ch // LANES
    hb = ts // CONV_HALO
    n_hblk = s // CONV_HALO

    def chunked(v):
        return v.reshape(nc, 1, LANES)

    w = dw_w.reshape(CONV_WIDTH, nc, LANES).transpose(1, 0, 2)
    w = jnp.pad(w, ((0, 0), (0, 32 - CONV_WIDTH), (0, 0)))
    vec_spec = pl.BlockSpec((nc, 1, LANES), lambda i: (0, 0, 0))
    kern = functools.partial(_conv_kernel, ts=ts, nc=nc, rb=64)
    return pl.pallas_call(
        kern,
        grid=(s // ts,),
        in_specs=[pl.BlockSpec((ts, n_ch), lambda i: (i, 0)),
                  pl.BlockSpec((ts, n_ch), lambda i: (i, 1)),
                  pl.BlockSpec((CONV_HALO, n_ch), lambda i: (jnp.maximum(i * hb - 1, 0), 0)),
                  pl.BlockSpec((CONV_HALO, n_ch), lambda i: (jnp.maximum(i * hb - 1, 0), 1)),
                  pl.BlockSpec((CONV_HALO, n_ch), lambda i: (jnp.minimum((i + 1) * hb, n_hblk - 1), 0)),
                  pl.BlockSpec((CONV_HALO, n_ch), lambda i: (jnp.minimum((i + 1) * hb, n_hblk - 1), 1)),
                  pl.BlockSpec((nc, 32, LANES), lambda i: (0, 0, 0)),
                  vec_spec, vec_spec, vec_spec, vec_spec],
        out_specs=pl.BlockSpec((ts, n_ch), lambda i: (i, 0)),
        out_shape=jax.ShapeDtypeStruct((s, n_ch), BF16),
        scratch_shapes=[pltpu.VMEM((nc, ts + 2 * CONV_HALO, LANES), F32),
                        pltpu.VMEM((nc, ts, LANES), F32)],
        compiler_params=_cparams("parallel"),
        name="conv_group",
    )(z, z, z, z, z, z, w, chunked(dw_b), chunked(ln_w), chunked(ln_b), chunked(on_w))


def _q_kernel(z_ref, nw_ref, w_ref, cos_ref, sin_ref, hw_ref, q_ref, zn_ref, *, q_scale):
    @pl.when(pl.program_id(1) == 0)
    def _():
        z = z_ref[...]
        zn_ref[...] = (z * _rms_scale(z, z.shape[-1]) * nw_ref[...]).astype(BF16)

    hw = hw_ref[...]
    for hh in range(w_ref.shape[0]):
        r = jnp.dot(zn_ref[...], w_ref[hh], preferred_element_type=F32)
        nope = r[:, :LANES]
        rope = r[:, LANES:2 * LANES] * cos_ref[...] + r[:, 2 * LANES:] * sin_ref[...]
        ss = jnp.sum(nope * nope, axis=-1, keepdims=True) + jnp.sum(rope * rope, axis=-1, keepdims=True)
        inv = lax.rsqrt(ss * (1.0 / QK_HEAD_DIM) + EPS) * q_scale
        q_ref[hh, :LANES, :] = (nope * inv * hw[:, :LANES]).T.astype(BF16)
        q_ref[hh, LANES:, :] = (rope * inv * hw[:, LANES:]).T.astype(BF16)


def _q_proj(z, nw, w3, cos_t, sin_t, hw, *, col_blk, tm, hb, q_scale):
    s = z.shape[0]
    rank = w3.shape[1]
    kern = functools.partial(_q_kernel, q_scale=q_scale)
    return pl.pallas_call(
        kern,
        grid=(s // tm, N_HEADS // hb),
        in_specs=[pl.BlockSpec((tm, rank), lambda i, h: (i, col_blk)),
                  pl.BlockSpec((1, rank), lambda i, h: (0, 0)),
                  pl.BlockSpec((hb, rank, 3 * LANES), lambda i, h: (h, 0, 0)),
                  pl.BlockSpec((tm, LANES), lambda i, h: (i, 0)),
                  pl.BlockSpec((tm, LANES), lambda i, h: (i, 0)),
                  pl.BlockSpec((1, QK_PAD_DIM), lambda i, h: (0, 0))],
        out_specs=pl.BlockSpec((hb, QK_PAD_DIM, tm), lambda i, h: (h, 0, i)),
        out_shape=jax.ShapeDtypeStruct((N_HEADS, QK_PAD_DIM, s), BF16),
        scratch_shapes=[pltpu.VMEM((tm, rank), BF16)],
        compiler_params=_cparams("parallel", "arbitrary"),
        name="q_proj",
    )(z, nw, w3, cos_t, sin_t, hw)


def _kv_kernel(z_ref, kpe_ref, kpes_ref, nw_ref, w_ref, cos_ref, sin_ref, hw_ref, k_ref, v_ref,
               zn_ref, kr_ref, ss_ref):
    @pl.when(pl.program_id(1) == 0)
    def _():
        z = z_ref[...]
        zn_ref[...] = (z * _rms_scale(z, z.shape[-1]) * nw_ref[...]).astype(BF16)
        kr = kpe_ref[...] * cos_ref[...] + kpes_ref[...] * sin_ref[...]
        kr_ref[...] = kr
        ss_ref[...] = jnp.sum(kr * kr, axis=-1, keepdims=True)

    hw = hw_ref[...]
    for hh in range(w_ref.shape[0]):
        r = jnp.dot(zn_ref[...], w_ref[hh], preferred_element_type=F32)
        kn = r[:, :LANES]
        ss = jnp.sum(kn * kn, axis=-1, keepdims=True) + ss_ref[...]
        inv = lax.rsqrt(ss * (1.0 / QK_HEAD_DIM) + EPS)
        k_ref[hh, :, :LANES] = (kn * inv * hw[:, :LANES]).astype(BF16)
        k_ref[hh, :, LANES:] = (kr_ref[...] * inv * hw[:, LANES:]).astype(BF16)
        v_ref[hh] = r[:, LANES:].T.astype(BF16)


def _kv_proj(z, nw, w3, cos_t, sin_t, hw, *, ckv_blk, kpe_blk, tm, hb):
    s = z.shape[0]
    rank = w3.shape[1]
    return pl.pallas_call(
        _kv_kernel,
        grid=(s // tm, N_HEADS // hb),
        in_specs=[pl.BlockSpec((tm, rank), lambda i, h: (i, ckv_blk)),
                  pl.BlockSpec((tm, LANES), lambda i, h: (i, kpe_blk)),
                  pl.BlockSpec((tm, LANES), lambda i, h: (i, kpe_blk + 1)),
                  pl.BlockSpec((1, rank), lambda i, h: (0, 0)),
                  pl.BlockSpec((hb, rank, 2 * LANES), lambda i, h: (h, 0, 0)),
                  pl.BlockSpec((tm, LANES), lambda i, h: (i, 0)),
                  pl.BlockSpec((tm, LANES), lambda i, h: (i, 0)),
                  pl.BlockSpec((1, QK_PAD_DIM), lambda i, h: (0, 0))],
        out_specs=[pl.BlockSpec((hb, tm, QK_PAD_DIM), lambda i, h: (h, i, 0)),
                   pl.BlockSpec((hb, V_HEAD_DIM, tm), lambda i, h: (h, 0, i))],
        out_shape=[jax.ShapeDtypeStruct((N_HEADS, s, QK_PAD_DIM), BF16),
                   jax.ShapeDtypeStruct((N_HEADS, V_HEAD_DIM, s), BF16)],
        scratch_shapes=[pltpu.VMEM((tm, rank), BF16),
                        pltpu.VMEM((tm, LANES), F32),
                        pltpu.VMEM((tm, 1), F32)],
        compiler_params=_cparams("parallel", "arbitrary"),
        name="kv_proj",
    )(z, z, z, nw, w3, cos_t, sin_t, hw)


def _attn_kernel(qt_ref, k_ref, vt_ref, o_ref, st_ref, *, tk):
    qt = qt_ref[0]
    tq = qt.shape[1]
    nk = k_ref.shape[1] // tk

    def scores(j, slot):
        start = pl.multiple_of(j * tk, tk)
        st_ref[slot] = jnp.dot(k_ref[0, pl.ds(start, tk), :], qt, preferred_element_type=F32)

    def update(j, slot, carry):
        m, l, acc = carry
        start = pl.multiple_of(j * tk, tk)
        vts = vt_ref[0, :, pl.ds(start, tk)]
        st = st_ref[slot]
        m_new = jnp.maximum(m, jnp.max(st, axis=0, keepdims=True))
        p = jnp.exp2(st - m_new)
        alpha = jnp.exp2(m - m_new)
        l = alpha * l + jnp.sum(p, axis=0, keepdims=True)
        acc = alpha * acc + jnp.dot(vts, p.astype(BF16), preferred_element_type=F32)
        return m_new, l, acc

    def body(jj, carry):
        j = 2 * jj
        scores(j + 1, 1)
        carry = update(j, 0, carry)
        scores(jnp.minimum(j + 2, nk - 1), 0)
        return update(j + 1, 1, carry)

    m0 = jnp.full((1, tq), -jnp.inf, F32)
    l0 = jnp.zeros((1, tq), F32)
    acc0 = jnp.zeros((V_HEAD_DIM, tq), F32)
    scores(0, 0)
    _, l, acc = lax.fori_loop(0, nk // 2, body, (m0, l0, acc0), unroll=2)
    o_ref[...] = (acc / l).T


def _attention(qt, k, vt, *, tq, tk):
    h, s, _ = k.shape
    kern = functools.partial(_attn_kernel, tk=tk)
    return pl.pallas_call(
        kern,
        grid=(h, s // tq),
        in_specs=[pl.BlockSpec((1, QK_PAD_DIM, tq), lambda hh, i: (hh, 0, i)),
                  pl.BlockSpec((1, s, QK_PAD_DIM), lambda hh, i: (hh, 0, 0)),
                  pl.BlockSpec((1, V_HEAD_DIM, s), lambda hh, i: (hh, 0, 0))],
        out_specs=pl.BlockSpec((tq, V_HEAD_DIM), lambda hh, i: (i, hh)),
        out_shape=jax.ShapeDtypeStruct((s, h * V_HEAD_DIM), F32),
        scratch_shapes=[pltpu.VMEM((2, tk, tq), F32)],
        compiler_params=_cparams("parallel", "arbitrary"),
        name="attention",
    )(qt, k, vt)


def _out_proj_kernel(a_ref, c_ref, x_ref, nw_ref, wa_ref, wc_ref, o_ref, an_ref):
    @pl.when(pl.program_id(1) == 0)
    def _():
        a = a_ref[...]
        an_ref[...] = (a * _rms_scale(a, a.shape[-1]) * nw_ref[...]).astype(BF16)

    acc = jnp.dot(an_ref[...], wa_ref[...], preferred_element_type=F32)
    acc = acc + jnp.dot(c_ref[...], wc_ref[...], preferred_element_type=F32)
    o_ref[...] = x_ref[...] + acc


def _out_proj(attn, conv_n, x, nw, w_o, *, tm, tn):
    s, da = attn.shape
    dc = conv_n.shape[1]
    d = x.shape[1]
    assert da == dc, "the two head groups share one W_o row-block size"
    return pl.pallas_call(
        _out_proj_kernel,
        grid=(s // tm, d // tn),
        in_specs=[pl.BlockSpec((tm, da), lambda i, j: (i, 0)),
                  pl.BlockSpec((tm, dc), lambda i, j: (i, 0)),
                  pl.BlockSpec((tm, tn), lambda i, j: (i, j)),
                  pl.BlockSpec((1, da), lambda i, j: (0, 0)),
                  pl.BlockSpec((da, tn), lambda i, j: (0, j)),
                  pl.BlockSpec((dc, tn), lambda i, j: (1, j))],
        out_specs=pl.BlockSpec((tm, tn), lambda i, j: (i, j)),
        out_shape=jax.ShapeDtypeStruct((s, d), F32),
        scratch_shapes=[pltpu.VMEM((tm, da), BF16)],
        compiler_params=_cparams("parallel", "arbitrary"),
        name="out_proj",
    )(attn, conv_n, x, nw, w_o, w_o)


def _router_kernel(x_ref, nw_ref, wr_ref, br_ref, tri_ref, h_ref, idx_ref, gate_ref, rank_ref, cnt_ref, run_ref):
    @pl.when(pl.program_id(0) == 0)
    def _():
        run_ref[...] = jnp.zeros_like(run_ref)

    x = x_ref[...]
    h = x * _rms_scale(x, x.shape[-1]) * nw_ref[...]
    half = h.shape[1] // 2
    lo = lax.bitcast_convert_type(h[:, :half].astype(BF16).astype(F32), U32)
    hi = lax.bitcast_convert_type(h[:, half:].astype(BF16).astype(F32), U32)
    h_ref[...] = (lo >> 16) | hi
    tm = x.shape[0]
    logits = lax.dot_general(wr_ref[...], h, (((1,), (1,)), ((), ())), preferred_element_type=F32,
                             precision=lax.Precision.HIGHEST) + br_ref[...]
    eio = lax.broadcasted_iota(I32, (N_EXPERTS, tm), 0)
    vals, idxs = [], []
    cur = logits
    for _ in range(TOP_K):
        m = jnp.max(cur, axis=0, keepdims=True)
        idx = jnp.min(jnp.where(cur == m, eio, N_EXPERTS), axis=0, keepdims=True)
        vals.append(m)
        idxs.append(idx)
        cur = jnp.where(eio == idx, -jnp.inf, cur)
    es = [jnp.exp(v - vals[0]) for v in vals]
    den = es[0] + es[1] + es[2] + es[3]
    base = run_ref[...]
    for k in range(TOP_K):
        gate_ref[k:k + 1, :] = es[k] / den
        idx_ref[k:k + 1, :] = idxs[k]
        onehot = eio == idxs[k]
        before = jnp.dot(onehot.astype(BF16), tri_ref[...], preferred_element_type=F32)
        rank = jnp.sum(jnp.where(onehot, before + base, 0.0), axis=0, keepdims=True)
        rank_ref[k:k + 1, :] = rank.astype(I32)
        base = base + jnp.sum(onehot.astype(F32), axis=1, keepdims=True)
    run_ref[...] = base
    cnt_ref[...] = base.astype(I32)


def _router(x1, nw, wr_t, br, *, tm):
    t, d = x1.shape
    tri = (lax.broadcasted_iota(I32, (tm, tm), 0) < lax.broadcasted_iota(I32, (tm, tm), 1)).astype(BF16)
    row4 = pl.BlockSpec((TOP_K, tm), lambda i: (0, i))
    return pl.pallas_call(
        _router_kernel,
        grid=(t // tm,),
        in_specs=[pl.BlockSpec((tm, d), lambda i: (i, 0)),
                  pl.BlockSpec((1, d), lambda i: (0, 0)),
                  pl.BlockSpec((N_EXPERTS, d), lambda i: (0, 0)),
                  pl.BlockSpec((N_EXPERTS, 1), lambda i: (0, 0)),
                  pl.BlockSpec((tm, tm), lambda i: (0, 0))],
        out_specs=[pl.BlockSpec((tm, d // 2), lambda i: (i, 0)), row4, row4, row4,
                   pl.BlockSpec((N_EXPERTS, 1), lambda i: (0, 0))],
        out_shape=[jax.ShapeDtypeStruct((t, d // 2), U32),
                   jax.ShapeDtypeStruct((TOP_K, t), I32),
                   jax.ShapeDtypeStruct((TOP_K, t), F32),
                   jax.ShapeDtypeStruct((TOP_K, t), I32),
                   jax.ShapeDtypeStruct((N_EXPERTS, 1), I32)],
        scratch_shapes=[pltpu.VMEM((N_EXPERTS, 1), F32)],
        compiler_params=_cparams("arbitrary"),
        name="router",
    )(x1, nw, wr_t, br, tri)


def _gather_kernel(dest_ref, h_ref, xs_in_hbm, xs_hbm, sem, *, tt):
    del xs_in_hbm

    def row_copy(src_row, dst_row):
        return pltpu.make_async_copy(h_ref.at[pl.ds(src_row, 1)], xs_hbm.at[pl.ds(dst_row, 1)], sem)

    def start_body(t, carry):
        for k in range(TOP_K):
            row_copy(t, dest_ref[0, k, t]).start()
        return carry

    def wait_body(t, carry):
        for k in range(TOP_K):
            row_copy(0, 0).wait()
        return carry

    lax.fori_loop(0, tt, start_body, 0)
    lax.fori_loop(0, tt, wait_body, 0)


def _moe_gather(h2, dest, n_rows, *, tt):
    t, d = h2.shape
    dest3 = dest.reshape(TOP_K, t // tt, tt).transpose(1, 0, 2)
    xs0 = jnp.zeros((n_rows, d), h2.dtype)
    kern = functools.partial(_gather_kernel, tt=tt)
    return pl.pallas_call(
        kern,
        grid=(t // tt,),
        in_specs=[pl.BlockSpec((1, TOP_K, tt), lambda i: (i, 0, 0), memory_space=pltpu.SMEM),
                  pl.BlockSpec((tt, d), lambda i: (i, 0)),
                  pl.BlockSpec(memory_space=pl.ANY)],
        out_specs=pl.BlockSpec(memory_space=pl.ANY),
        out_shape=jax.ShapeDtypeStruct((n_rows, d), h2.dtype),
        scratch_shapes=[pltpu.SemaphoreType.DMA(())],
        input_output_aliases={2: 0},
        compiler_params=_cparams("arbitrary"),
        name="moe_gather",
    )(dest3, h2, xs0)


def _expert_changed(te_ref, i):
    return jnp.logical_or(i == 0, te_ref[i] != te_ref[jnp.maximum(i - 1, 0)])


def _expert_up_kernel(te_ref, nv_ref, x_ref, wg_ref, bg_ref, wu_ref, bu_ref, o_ref, wgc_ref, wuc_ref):
    i = pl.program_id(1)
    valid = i < nv_ref[0]

    @pl.when(jnp.logical_and(valid, _expert_changed(te_ref, i)))
    def _():
        wgc_ref[...] = wg_ref[0].astype(BF16)
        wuc_ref[...] = wu_ref[0].astype(BF16)

    @pl.when(valid)
    def _():
        xw = x_ref[...]
        half = xw.shape[1]
        x_lo = lax.bitcast_convert_type(xw << 16, F32).astype(BF16)
        x_hi = lax.bitcast_convert_type((xw >> 16) << 16, F32).astype(BF16)

        def proj(w_ref, b_ref):
            y = jnp.dot(x_lo, w_ref[:half, :], preferred_element_type=F32)
            return y + jnp.dot(x_hi, w_ref[half:, :], preferred_element_type=F32) + b_ref[0]

        gate = proj(wgc_ref, bg_ref)
        lin = proj(wuc_ref, bu_ref)
        gate = jnp.minimum(gate, SWIGLU_LIMIT)
        lin = jnp.clip(lin, -SWIGLU_LIMIT, SWIGLU_LIMIT)
        act = gate * jax.nn.sigmoid(SWIGLU_ALPHA * gate) * (lin + 1.0)
        o_ref[...] = act.astype(BF16)

    @pl.when(jnp.logical_not(valid))
    def _():
        o_ref[...] = jnp.zeros_like(o_ref)


def _expert_up(tile_e, n_valid, xs, wg, bg, wu, bu, *, tm, tn):
    r, dw = xs.shape
    d, de = wg.shape[1], wg.shape[2]
    assert 2 * dw == d, "xs rows hold two bf16 columns per 32-bit word"

    def row_blk(j, i, te, nv):
        return (jnp.minimum(i, jnp.maximum(nv[0] - 1, 0)), 0)

    def w_blk(j, i, te, nv):
        return (te[i], 0, j)

    return pl.pallas_call(
        _expert_up_kernel,
        grid_spec=pltpu.PrefetchScalarGridSpec(
            num_scalar_prefetch=2,
            grid=(de // tn, r // tm),
            in_specs=[pl.BlockSpec((tm, dw), row_blk),
                      pl.BlockSpec((1, d, tn), w_blk),
                      pl.BlockSpec((1, 1, tn), w_blk),
                      pl.BlockSpec((1, d, tn), w_blk),
                      pl.BlockSpec((1, 1, tn), w_blk)],
            out_specs=pl.BlockSpec((tm, tn), lambda j, i, te, nv: (i, j)),
            scratch_shapes=[pltpu.VMEM((d, tn), BF16), pltpu.VMEM((d, tn), BF16)],
        ),
        out_shape=jax.ShapeDtypeStruct((r, de), BF16),
        compiler_params=_cparams("arbitrary", "arbitrary"),
        name="expert_up",
    )(tile_e, n_valid, xs, wg, bg, wu, bu)


def _expert_down_kernel(te_ref, nv_ref, a_ref, w_ref, b_ref, o_ref, wc_ref):
    i = pl.program_id(1)
    valid = i < nv_ref[0]

    @pl.when(jnp.logical_and(valid, _expert_changed(te_ref, i)))
    def _():
        wc_ref[...] = w_ref[0].astype(BF16)

    @pl.when(valid)
    def _():
        o_ref[...] = jnp.dot(a_ref[...], wc_ref[...], preferred_element_type=F32) + b_ref[0]

    @pl.when(jnp.logical_not(valid))
    def _():
        o_ref[...] = jnp.zeros_like(o_ref)


def _expert_down(tile_e, n_valid, act, wd, bd, *, tm, tn):
    r, de = act.shape
    d = wd.shape[2]

    def row_blk(j, i, te, nv):
        return (jnp.minimum(i, jnp.maximum(nv[0] - 1, 0)), 0)

    def w_blk(j, i, te, nv):
        return (te[i], 0, j)

    return pl.pallas_call(
        _expert_down_kernel,
        grid_spec=pltpu.PrefetchScalarGridSpec(
            num_scalar_prefetch=2,
            grid=(d // tn, r // tm),
            in_specs=[pl.BlockSpec((tm, de), row_blk),
                      pl.BlockSpec((1, de, tn), w_blk),
                      pl.BlockSpec((1, 1, tn), w_blk)],
            out_specs=pl.BlockSpec((tm, tn), lambda j, i, te, nv: (i, j)),
            scratch_shapes=[pltpu.VMEM((de, tn), BF16)],
        ),
        out_shape=jax.ShapeDtypeStruct((r, d), F32),
        compiler_params=_cparams("arbitrary", "arbitrary"),
        name="expert_down",
    )(tile_e, n_valid, act, wd, bd)


def _combine_kernel(dest_ref, x_ref, g_ref, ys_hbm, o_ref, buf_ref, sem, *, tt):
    def row_copy(src_row, k, t):
        return pltpu.make_async_copy(ys_hbm.at[pl.ds(src_row, 1)], buf_ref.at[k, pl.ds(t, 1)], sem)

    def start_body(t, carry):
        for k in range(TOP_K):
            row_copy(dest_ref[0, k, t], k, t).start()
        return carry

    def wait_body(t, carry):
        for k in range(TOP_K):
            row_copy(0, k, t).wait()
        return carry

    lax.fori_loop(0, tt, start_body, 0)
    lax.fori_loop(0, tt, wait_body, 0)
    g = g_ref[...]
    acc = x_ref[...]
    for k in range(TOP_K):
        acc = acc + g[:, k:k + 1] * buf_ref[k]
    o_ref[...] = acc


def _moe_combine(dest, x1, gates_t, ys, *, tt):
    t, d = x1.shape
    dest3 = dest.reshape(TOP_K, t // tt, tt).transpose(1, 0, 2)
    kern = functools.partial(_combine_kernel, tt=tt)
    return pl.pallas_call(
        kern,
        grid=(t // tt,),
        in_specs=[pl.BlockSpec((1, TOP_K, tt), lambda i: (i, 0, 0), memory_space=pltpu.SMEM),
                  pl.BlockSpec((tt, d), lambda i: (i, 0)),
                  pl.BlockSpec((tt, TOP_K), lambda i: (i, 0)),
                  pl.BlockSpec(memory_space=pl.ANY)],
        out_specs=pl.BlockSpec((tt, d), lambda i: (i, 0)),
        out_shape=jax.ShapeDtypeStruct((t, d), F32),
        scratch_shapes=[pltpu.VMEM((TOP_K, tt, d), F32), pltpu.SemaphoreType.DMA(())],
        compiler_params=_cparams("arbitrary"),
        name="moe_combine",
    )(dest3, x1, gates_t, ys)


def _rope_tables(positions):
    half = QK_ROPE_DIM // 2
    inv_freq = ROPE_THETA ** (-jnp.arange(0, QK_ROPE_DIM, 2, dtype=F32) / QK_ROPE_DIM)
    ang = positions.astype(F32)[:, None] * inv_freq
    cos, sin = jnp.cos(ang), jnp.sin(ang)
    zeros = jnp.zeros((positions.shape[0], LANES - 2 * half), F32)
    return jnp.concatenate([cos, cos, zeros], axis=1), jnp.concatenate([-sin, sin, zeros], axis=1)


def _swap_halves(w):
    half = w.shape[-1] // 2
    return jnp.concatenate([w[..., half:], w[..., :half]], axis=-1)


def _layer(x, positions, attn_norm_w, w_in, q_a_norm_w, w_uq, kv_a_norm_w, w_ukv, q_head_norm_w, k_head_norm_w,
           conv_dw_w, conv_dw_b, conv_ln_w, conv_ln_b, attn_out_norm_w, conv_out_norm_w, w_o, ffn_norm_w,
           w_router, b_router, w_gate, b_gate, w_up, b_up, w_down, b_down):
    s, d = x.shape
    q_rank = q_a_norm_w.shape[0]
    kv_rank = kv_a_norm_w.shape[0]
    n_conv = conv_dw_b.shape[0]
    glu_w = 2 * n_conv
    de = w_gate.shape[2]

    pad64 = jnp.zeros((d, LANES - QK_ROPE_DIM), F32)
    w_kpe = w_in[:, glu_w + q_rank + kv_rank:]
    z_cols = glu_w + q_rank + kv_rank + 2 * LANES
    in_tn = 1024
    z_pad = -z_cols % in_tn
    w_in_p = jnp.concatenate(
        [w_in[:, :glu_w + q_rank + kv_rank], w_kpe, pad64, _swap_halves(w_kpe), pad64, jnp.zeros((d, z_pad), F32)],
        axis=1).astype(BF16)

    wq = w_uq.reshape(q_rank, N_HEADS, QK_HEAD_DIM)
    wq_pe = wq[:, :, QK_NOPE_DIM:]
    zq = jnp.zeros((q_rank, N_HEADS, LANES - QK_ROPE_DIM), F32)
    wq3 = jnp.concatenate([wq[:, :, :QK_NOPE_DIM], wq_pe, zq, _swap_halves(wq_pe), zq], axis=2)
    wq3 = wq3.transpose(1, 0, 2).astype(BF16)
    wkv3 = w_ukv.reshape(kv_rank, N_HEADS, QK_NOPE_DIM + V_HEAD_DIM).transpose(1, 0, 2).astype(BF16)
    zero_hw = jnp.zeros((QK_PAD_DIM - QK_HEAD_DIM,), F32)
    q_hw = jnp.concatenate([q_head_norm_w, zero_hw])[None]
    k_hw = jnp.concatenate([k_head_norm_w, zero_hw])[None]
    cos_t, sin_t = _rope_tables(positions)

    z = _norm_matmul(x, attn_norm_w[None], w_in_p, tm=512, tn=in_tn)
    conv_n = _conv_group(z, conv_dw_w, conv_dw_b, conv_ln_w, conv_ln_b, conv_out_norm_w, n_ch=n_conv, ts=256)
    q_scale = QK_HEAD_DIM ** -0.5 * LOG2E
    q = _q_proj(z, q_a_norm_w[None], wq3, cos_t, sin_t, q_hw, col_blk=glu_w // q_rank, tm=512, hb=4,
                q_scale=q_scale)
    k, v = _kv_proj(z, kv_a_norm_w[None], wkv3, cos_t, sin_t, k_hw, ckv_blk=(glu_w + q_rank) // kv_rank,
                    kpe_blk=(glu_w + q_rank + kv_rank) // LANES, tm=512, hb=4)
    attn = _attention(q, k, v, tq=512, tk=512)
    x1 = _out_proj(attn, conv_n, x, attn_out_norm_w[None], w_o.astype(BF16), tm=512, tn=1024)

    tm_e = 256
    h2, top_idx, gates, rank, counts = _router(x1, ffn_norm_w[None], w_router.T, b_router[:, None], tm=512)
    counts = counts[:, 0]
    padded = (counts + tm_e - 1) // tm_e * tm_e
    pad_end = jnp.cumsum(padded)
    pad_start = pad_end - padded
    expert_ids = jnp.arange(N_EXPERTS, dtype=I32)[:, None, None]
    dest = jnp.sum(jnp.where(top_idx[None] == expert_ids, pad_start[:, None, None], 0), axis=0) + rank
    n_rows = s * TOP_K + N_EXPERTS * tm_e
    n_tiles = n_rows // tm_e
    n_valid = pad_end[-1] // tm_e
    tile_ids = jnp.arange(n_tiles, dtype=I32)
    tile_e = jnp.clip(jnp.searchsorted(pad_end, tile_ids * tm_e, side='right'), 0, N_EXPERTS - 1).astype(I32)
    tile_e = jnp.where(tile_ids < n_valid, tile_e, tile_e[n_valid - 1])
    n_valid = n_valid.astype(I32)[None]

    xs = _moe_gather(h2, dest, n_rows, tt=512)
    act = _expert_up(tile_e, n_valid, xs, w_gate, b_gate[:, None, :], w_up, b_up[:, None, :], tm=tm_e, tn=512)
    ys = _expert_down(tile_e, n_valid, act, w_down, b_down[:, None, :], tm=tm_e, tn=2048)
    return _moe_combine(dest, x1, gates.T, ys, tt=128)


def kernel(x, positions, attn_norm_w, w_in, q_a_norm_w, w_uq, kv_a_norm_w, w_ukv, q_head_norm_w, k_head_norm_w,
           conv_dw_w, conv_dw_b, conv_ln_w, conv_ln_b, attn_out_norm_w, conv_out_norm_w, w_o, ffn_norm_w,
           w_router, b_router, w_gate, b_gate, w_up, b_up, w_down, b_down):
    b, s, d = x.shape
    depth = attn_norm_w.shape[0]
    assert b == 1, "one sequence per call"
    xs = x.reshape(s, d)
    for l in range(depth):
        xs = _layer(xs, positions[0], attn_norm_w[l], w_in[l], q_a_norm_w[l], w_uq[l], kv_a_norm_w[l], w_ukv[l],
                    q_head_norm_w[l], k_head_norm_w[l], conv_dw_w[l][:, 0, :], conv_dw_b[l], conv_ln_w[l],
                    conv_ln_b[l], attn_out_norm_w[l], conv_out_norm_w[l], w_o[l], ffn_norm_w[l], w_router[l],
                    b_router[l], w_gate[l], b_gate[l], w_up[l], b_up[l], w_down[l], b_down[l])
    return xs.reshape(b, s, d)
```

```python
import functools

import jax
import jax.numpy as jnp
from jax import lax
from jax.experimental import pallas as pl
from jax.experimental.pallas import tpu as pltpu

F32, BF16, I32, U32 = jnp.float32, jnp.bfloat16, jnp.int32, jnp.uint32

N_HEADS = 16
QK_NOPE_DIM = 128
QK_ROPE_DIM = 64
QK_HEAD_DIM = QK_NOPE_DIM + QK_ROPE_DIM
V_HEAD_DIM = 128
QK_PAD_DIM = 256
ROPE_THETA = 10000.0
CONV_WIDTH = 31
N_EXPERTS = 32
TOP_K = 4
SWIGLU_LIMIT = 7.0
SWIGLU_ALPHA = 1.702
EPS = 1e-6
LOG2E = 1.4426950408889634

LANES = 128
CONV_HALO = 16
VMEM_LIMIT_BYTES = 56 * 1024 * 1024


def _cparams(*sem):
    return pltpu.CompilerParams(dimension_semantics=sem, vmem_limit_bytes=VMEM_LIMIT_BYTES)


def _rms_scale(x, n):
    return lax.rsqrt(jnp.sum(x * x, axis=-1, keepdims=True) * (1.0 / n) + EPS)


def _norm_matmul_kernel(x_ref, nw_ref, w_ref, o_ref, h_ref):
    @pl.when(pl.program_id(1) == 0)
    def _():
        x = x_ref[...]
        h_ref[...] = (x * _rms_scale(x, x.shape[-1]) * nw_ref[...]).astype(BF16)

    o_ref[...] = jnp.dot(h_ref[...], w_ref[...], preferred_element_type=F32)


def _norm_matmul(x, nw, w, *, tm, tn):
    m, k = x.shape
    n = w.shape[1]
    return pl.pallas_call(
        _norm_matmul_kernel,
        grid=(m // tm, n // tn),
        in_specs=[pl.BlockSpec((tm, k), lambda i, j: (i, 0)),
                  pl.BlockSpec((1, k), lambda i, j: (0, 0)),
                  pl.BlockSpec((k, tn), lambda i, j: (0, j))],
        out_specs=pl.BlockSpec((tm, tn), lambda i, j: (i, j)),
        out_shape=jax.ShapeDtypeStruct((m, n), F32),
        scratch_shapes=[pltpu.VMEM((tm, k), BF16)],
        compiler_params=_cparams("parallel", "arbitrary"),
        name="in_proj",
    )(x, nw, w)


def _conv_kernel(a_ref, g_ref, ap_ref, gp_ref, an_ref, gn_ref, w_ref, b_ref, lnw_ref, lnb_ref, onw_ref,
                 o_ref, uext_ref, conv_ref, *, ts, nc, rb):
    i = pl.program_id(0)
    has_prev = i > 0
    has_next = i < pl.num_programs(0) - 1
    halo = CONV_HALO

    def glu(a, g):
        return a * jax.nn.sigmoid(g)

    for c in range(nc):
        sl = slice(c * LANES, (c + 1) * LANES)
        uext_ref[c, 0:halo, :] = jnp.where(has_prev, glu(ap_ref[:, sl], gp_ref[:, sl]), 0.0)
        uext_ref[c, halo:halo + ts, :] = glu(a_ref[:, sl], g_ref[:, sl])
        uext_ref[c, halo + ts:halo + ts + halo, :] = jnp.where(has_next, glu(an_ref[:, sl], gn_ref[:, sl]), 0.0)

    off0 = halo - CONV_WIDTH // 2

    def chunk_body(c, carry):
        w = w_ref[c]
        wk = [jnp.broadcast_to(w[k:k + 1, :], (rb, LANES)) for k in range(CONV_WIDTH)]
        bias = jnp.broadcast_to(b_ref[c], (rb, LANES))
        for r in range(ts // rb):
            acc = bias
            for k in range(CONV_WIDTH):
                acc = acc + wk[k] * uext_ref[c, r * rb + off0 + k:r * rb + off0 + k + rb, :]
            conv_ref[c, r * rb:(r + 1) * rb, :] = acc
        return carry

    lax.fori_loop(0, nc, chunk_body, 0)

    n_ch = nc * LANES
    xc = conv_ref[...]
    mu = jnp.sum(jnp.sum(xc, axis=0), axis=-1, keepdims=True) * (1.0 / n_ch)
    d = xc - mu[None]
    var = jnp.sum(jnp.sum(d * d, axis=0), axis=-1, keepdims=True) * (1.0 / n_ch)
    y = d * lax.rsqrt(var + EPS)[None] * lnw_ref[...] + lnb_ref[...]
    y = y * jax.nn.sigmoid(y)
    ms = jnp.sum(jnp.sum(y * y, axis=0), axis=-1, keepdims=True) * (1.0 / n_ch)
    o = y * lax.rsqrt(ms + EPS)[None] * onw_ref[...]
    for c in range(nc):
        o_ref[:, c * LANES:(c + 1) * LANES] = o[c].astype(BF16)


def _conv_group(z, dw_w, dw_b, ln_w, ln_b, on_w, *, n_ch, ts):
    s = z.shape[0]
    nc = n_ch // LANES
    hb = ts // CONV_HALO
    n_hblk = s // CONV_HALO

    def chunked(v):
        return v.reshape(nc, 1, LANES)

    w = dw_w.reshape(CONV_WIDTH, nc, LANES).transpose(1, 0, 2)
    w = jnp.pad(w, ((0, 0), (0, 32 - CONV_WIDTH), (0, 0)))
    vec_spec = pl.BlockSpec((nc, 1, LANES), lambda i: (0, 0, 0))
    kern = functools.partial(_conv_kernel, ts=ts, nc=nc, rb=64)
    return pl.pallas_call(
        kern,
        grid=(s // ts,),
        in_specs=[pl.BlockSpec((ts, n_ch), lambda i: (i, 0)),
                  pl.BlockSpec((ts, n_ch), lambda i: (i, 1)),
                  pl.BlockSpec((CONV_HALO, n_ch), lambda i: (jnp.maximum(i * hb - 1, 0), 0)),
                  pl.BlockSpec((CONV_HALO, n_ch), lambda i: (jnp.maximum(i * hb - 1, 0), 1)),
                  pl.BlockSpec((CONV_HALO, n_ch), lambda i: (jnp.minimum((i + 1) * hb, n_hblk - 1), 0)),
                  pl.BlockSpec((CONV_HALO, n_ch), lambda i: (jnp.minimum((i + 1) * hb, n_hblk - 1), 1)),
                  pl.BlockSpec((nc, 32, LANES), lambda i: (0, 0, 0)),
                  vec_spec, vec_spec, vec_spec, vec_spec],
        out_specs=pl.BlockSpec((ts, n_ch), lambda i: (i, 0)),
        out_shape=jax.ShapeDtypeStruct((s, n_ch), BF16),
        scratch_shapes=[pltpu.VMEM((nc, ts + 2 * CONV_HALO, LANES), F32),
                        pltpu.VMEM((nc, ts, LANES), F32)],
        compiler_params=_cparams("parallel"),
        name="conv_group",
    )(z, z, z, z, z, z, w, chunked(dw_b), chunked(ln_w), chunked(ln_b), chunked(on_w))


def _q_kernel(z_ref, nw_ref, w_ref, cos_ref, sin_ref, hw_ref, q_ref, zn_ref, *, q_scale):
    @pl.when(pl.program_id(1) == 0)
    def _():
        z = z_ref[...]
        zn_ref[...] = (z * _rms_scale(z, z.shape[-1]) * nw_ref[...]).astype(BF16)

    hw = hw_ref[...]
    for hh in range(w_ref.shape[0]):
        r = jnp.dot(zn_ref[...], w_ref[hh], preferred_element_type=F32)
        nope = r[:, :LANES]
        rope = r[:, LANES:2 * LANES] * cos_ref[...] + r[:, 2 * LANES:] * sin_ref[...]
        ss = jnp.sum(nope * nope, axis=-1, keepdims=True) + jnp.sum(rope * rope, axis=-1, keepdims=True)
        inv = lax.rsqrt(ss * (1.0 / QK_HEAD_DIM) + EPS) * q_scale
        q_ref[hh, :LANES, :] = (nope * inv * hw[:, :LANES]).T.astype(BF16)
        q_ref[hh, LANES:, :] = (rope * inv * hw[:, LANES:]).T.astype(BF16)


def _q_proj(z, nw, w3, cos_t, sin_t, hw, *, col_blk, tm, hb, q_scale):
    s = z.shape[0]
    rank = w3.shape[1]
    kern = functools.partial(_q_kernel, q_scale=q_scale)
    return pl.pallas_call(
        kern,
        grid=(s // tm, N_HEADS // hb),
        in_specs=[pl.BlockSpec((tm, rank), lambda i, h: (i, col_blk)),
                  pl.BlockSpec((1, rank), lambda i, h: (0, 0)),
                  pl.BlockSpec((hb, rank, 3 * LANES), lambda i, h: (h, 0, 0)),
                  pl.BlockSpec((tm, LANES), lambda i, h: (i, 0)),
                  pl.BlockSpec((tm, LANES), lambda i, h: (i, 0)),
                  pl.BlockSpec((1, QK_PAD_DIM), lambda i, h: (0, 0))],
        out_specs=pl.BlockSpec((hb, QK_PAD_DIM, tm), lambda i, h: (h, 0, i)),
        out_shape=jax.ShapeDtypeStruct((N_HEADS, QK_PAD_DIM, s), BF16),
        scratch_shapes=[pltpu.VMEM((tm, rank), BF16)],
        compiler_params=_cparams("parallel", "arbitrary"),
        name="q_proj",
    )(z, nw, w3, cos_t, sin_t, hw)


def _kv_kernel(z_ref, kpe_ref, kpes_ref, nw_ref, w_ref, cos_ref, sin_ref, hw_ref, k_ref, v_ref,
               zn_ref, kr_ref, ss_ref):
    @pl.when(pl.program_id(1) == 0)
    def _():
        z = z_ref[...]
        zn_ref[...] = (z * _rms_scale(z, z.shape[-1]) * nw_ref[...]).astype(BF16)
        kr = kpe_ref[...] * cos_ref[...] + kpes_ref[...] * sin_ref[...]
        kr_ref[...] = kr
        ss_ref[...] = jnp.sum(kr * kr, axis=-1, keepdims=True)

    hw = hw_ref[...]
    for hh in range(w_ref.shape[0]):
        r = jnp.dot(zn_ref[...], w_ref[hh], preferred_element_type=F32)
        kn = r[:, :LANES]
        ss = jnp.sum(kn * kn, axis=-1, keepdims=True) + ss_ref[...]
        inv = lax.rsqrt(ss * (1.0 / QK_HEAD_DIM) + EPS)
        k_ref[hh, :, :LANES] = (kn * inv * hw[:, :LANES]).astype(BF16)
        k_ref[hh, :, LANES:] = (kr_ref[...] * inv * hw[:, LANES:]).astype(BF16)
        v_ref[hh] = r[:, LANES:].T.astype(BF16)


def _kv_proj(z, nw, w3, cos_t, sin_t, hw, *, ckv_blk, kpe_blk, tm, hb):
    s = z.shape[0]
    rank = w3.shape[1]
    return pl.pallas_call(
        _kv_kernel,
        grid=(s // tm, N_HEADS // hb),
        in_specs=[pl.BlockSpec((tm, rank), lambda i, h: (i, ckv_blk)),
                  pl.BlockSpec((tm, LANES), lambda i, h: (i, kpe_blk)),
                  pl.BlockSpec((tm, LANES), lambda i, h: (i, kpe_blk + 1)),
                  pl.BlockSpec((1, rank), lambda i, h: (0, 0)),
                  pl.BlockSpec((hb, rank, 2 * LANES), lambda i, h: (h, 0, 0)),
                  pl.BlockSpec((tm, LANES), lambda i, h: (i, 0)),
                  pl.BlockSpec((tm, LANES), lambda i, h: (i, 0)),
                  pl.BlockSpec((1, QK_PAD_DIM), lambda i, h: (0, 0))],
        out_specs=[pl.BlockSpec((hb, tm, QK_PAD_DIM), lambda i, h: (h, i, 0)),
                   pl.BlockSpec((hb, V_HEAD_DIM, tm), lambda i, h: (h, 0, i))],
        out_shape=[jax.ShapeDtypeStruct((N_HEADS, s, QK_PAD_DIM), BF16),
                   jax.ShapeDtypeStruct((N_HEADS, V_HEAD_DIM, s), BF16)],
        scratch_shapes=[pltpu.VMEM((tm, rank), BF16),
                        pltpu.VMEM((tm, LANES), F32),
                        pltpu.VMEM((tm, 1), F32)],
        compiler_params=_cparams("parallel", "arbitrary"),
        name="kv_proj",
    )(z, z, z, nw, w3, cos_t, sin_t, hw)


def _attn_kernel(qt_ref, k_ref, vt_ref, o_ref, st_ref, *, tk):
    qt = qt_ref[0]
    tq = qt.shape[1]
    nk = k_ref.shape[1] // tk

    def scores(j, slot):
        start = pl.multiple_of(j * tk, tk)
        st_ref[slot] = jnp.dot(k_ref[0, pl.ds(start, tk), :], qt, preferred_element_type=F32)

    def update(j, slot, carry):
        m, l, acc = carry
        start = pl.multiple_of(j * tk, tk)
        vts = vt_ref[0, :, pl.ds(start, tk)]
        st = st_ref[slot]
        m_new = jnp.maximum(m, jnp.max(st, axis=0, keepdims=True))
        p = jnp.exp2(st - m_new)
        alpha = jnp.exp2(m - m_new)
        l = alpha * l + jnp.sum(p, axis=0, keepdims=True)
        acc = alpha * acc + jnp.dot(vts, p.astype(BF16), preferred_element_type=F32)
        return m_new, l, acc

    def body(jj, carry):
        j = 2 * jj
        scores(j + 1, 1)
        carry = update(j, 0, carry)
        scores(jnp.minimum(j + 2, nk - 1), 0)
        return update(j + 1, 1, carry)

    m0 = jnp.full((1, tq), -jnp.inf, F32)
    l0 = jnp.zeros((1, tq), F32)
    acc0 = jnp.zeros((V_HEAD_DIM, tq), F32)
    scores(0, 0)
    _, l, acc = lax.fori_loop(0, nk // 2, body, (m0, l0, acc0), unroll=2)
    o_ref[...] = (acc / l).T


def _attention(qt, k, vt, *, tq, tk):
    h, s, _ = k.shape
    kern = functools.partial(_attn_kernel, tk=tk)
    return pl.pallas_call(
        kern,
        grid=(h, s // tq),
        in_specs=[pl.BlockSpec((1, QK_PAD_DIM, tq), lambda hh, i: (hh, 0, i)),
                  pl.BlockSpec((1, s, QK_PAD_DIM), lambda hh, i: (hh, 0, 0)),
                  pl.BlockSpec((1, V_HEAD_DIM, s), lambda hh, i: (hh, 0, 0))],
        out_specs=pl.BlockSpec((tq, V_HEAD_DIM), lambda hh, i: (i, hh)),
        out_shape=jax.ShapeDtypeStruct((s, h * V_HEAD_DIM), F32),
        scratch_shapes=[pltpu.VMEM((2, tk, tq), F32)],
        compiler_params=_cparams("parallel", "arbitrary"),
        name="attention",
    )(qt, k, vt)


def _out_proj_kernel(a_ref, c_ref, x_ref, nw_ref, wa_ref, wc_ref, o_ref, an_ref):
    @pl.when(pl.program_id(1) == 0)
    def _():
        a = a_ref[...]
        an_ref[...] = (a * _rms_scale(a, a.shape[-1]) * nw_ref[...]).astype(BF16)

    acc = jnp.dot(an_ref[...], wa_ref[...], preferred_element_type=F32)
    acc = acc + jnp.dot(c_ref[...], wc_ref[...], preferred_element_type=F32)
    o_ref[...] = x_ref[...] + acc


def _out_proj(attn, conv_n, x, nw, w_o, *, tm, tn):
    s, da = attn.shape
    dc = conv_n.shape[1]
    d = x.shape[1]
    assert da == dc, "the two head groups share one W_o row-block size"
    return pl.pallas_call(
        _out_proj_kernel,
        grid=(s // tm, d // tn),
        in_specs=[pl.BlockSpec((tm, da), lambda i, j: (i, 0)),
                  pl.BlockSpec((tm, dc), lambda i, j: (i, 0)),
                  pl.BlockSpec((tm, tn), lambda i, j: (i, j)),
                  pl.BlockSpec((1, da), lambda i, j: (0, 0)),
                  pl.BlockSpec((da, tn), lambda i, j: (0, j)),
                  pl.BlockSpec((dc, tn), lambda i, j: (1, j))],
        out_specs=pl.BlockSpec((tm, tn), lambda i, j: (i, j)),
        out_shape=jax.ShapeDtypeStruct((s, d), F32),
        scratch_shapes=[pltpu.VMEM((tm, da), BF16)],
        compiler_params=_cparams("parallel", "arbitrary"),
        name="out_proj",
    )(attn, conv_n, x, nw, w_o, w_o)


def _router_kernel(x_ref, nw_ref, wr_ref, br_ref, tri_ref, h_ref, idx_ref, gate_ref, rank_ref, cnt_ref, run_ref):
    @pl.when(pl.program_id(0) == 0)
    def _():
        run_ref[...] = jnp.zeros_like(run_ref)

    x = x_ref[...]
    h = x * _rms_scale(x, x.shape[-1]) * nw_ref[...]
    half = h.shape[1] // 2
    lo = lax.bitcast_convert_type(h[:, :half].astype(BF16).astype(F32), U32)
    hi = lax.bitcast_convert_type(h[:, half:].astype(BF16).astype(F32), U32)
    h_ref[...] = (lo >> 16) | hi
    tm = x.shape[0]
    logits = lax.dot_general(wr_ref[...], h, (((1,), (1,)), ((), ())), preferred_element_type=F32,
                             precision=lax.Precision.HIGHEST) + br_ref[...]
    eio = lax.broadcasted_iota(I32, (N_EXPERTS, tm), 0)
    vals, idxs = [], []
    cur = logits
    for _ in range(TOP_K):
        m = jnp.max(cur, axis=0, keepdims=True)
        idx = jnp.min(jnp.where(cur == m, eio, N_EXPERTS), axis=0, keepdims=True)
        vals.append(m)
        idxs.append(idx)
        cur = jnp.where(eio == idx, -jnp.inf, cur)
    es = [jnp.exp(v - vals[0]) for v in vals]
    den = es[0] + es[1] + es[2] + es[3]
    base = run_ref[...]
    for k in range(TOP_K):
        gate_ref[k:k + 1, :] = es[k] / den
        idx_ref[k:k + 1, :] = idxs[k]
        onehot = eio == idxs[k]
        before = jnp.dot(onehot.astype(BF16), tri_ref[...], preferred_element_type=F32)
        rank = jnp.sum(jnp.where(onehot, before + base, 0.0), axis=0, keepdims=True)
        rank_ref[k:k + 1, :] = rank.astype(I32)
        base = base + jnp.sum(onehot.astype(F32), axis=1, keepdims=True)
    run_ref[...] = base
    cnt_ref[...] = base.astype(I32)


def _router(x1, nw, wr_t, br, *, tm):
    t, d = x1.shape
    tri = (lax.broadcasted_iota(I32, (tm, tm), 0) < lax.broadcasted_iota(I32, (tm, tm), 1)).astype(BF16)
    row4 = pl.BlockSpec((TOP_K, tm), lambda i: (0, i))
    return pl.pallas_call(
        _router_kernel,
        grid=(t // tm,),
        in_specs=[pl.BlockSpec((tm, d), lambda i: (i, 0)),
                  pl.BlockSpec((1, d), lambda i: (0, 0)),
                  pl.BlockSpec((N_EXPERTS, d), lambda i: (0, 0)),
                  pl.BlockSpec((N_EXPERTS, 1), lambda i: (0, 0)),
                  pl.BlockSpec((tm, tm), lambda i: (0, 0))],
        out_specs=[pl.BlockSpec((tm, d // 2), lambda i: (i, 0)), row4, row4, row4,
                   pl.BlockSpec((N_EXPERTS, 1), lambda i: (0, 0))],
        out_shape=[jax.ShapeDtypeStruct((t, d // 2), U32),
                   jax.ShapeDtypeStruct((TOP_K, t), I32),
                   jax.ShapeDtypeStruct((TOP_K, t), F32),
                   jax.ShapeDtypeStruct((TOP_K, t), I32),
                   jax.ShapeDtypeStruct((N_EXPERTS, 1), I32)],
        scratch_shapes=[pltpu.VMEM((N_EXPERTS, 1), F32)],
        compiler_params=_cparams("arbitrary"),
        name="router",
    )(x1, nw, wr_t, br, tri)


def _gather_kernel(zflag_ref, dest_ref, h_ref, xs_hbm, zero_ref, sem, zsem, *, tt, tm):
    n_tiles = xs_hbm.shape[0] // tm

    @pl.when(pl.program_id(0) == 0)
    def _():
        zero_ref[...] = jnp.zeros_like(zero_ref)

        def zero_copy(g):
            return pltpu.make_async_copy(zero_ref, xs_hbm.at[pl.ds(pl.multiple_of(g * tm, tm), tm)], zsem)

        def zstart(g, carry):
            @pl.when(zflag_ref[g] != 0)
            def _():
                zero_copy(g).start()
            return carry

        def zwait(g, carry):
            @pl.when(zflag_ref[g] != 0)
            def _():
                zero_copy(g).wait()
            return carry

        lax.fori_loop(0, n_tiles, zstart, 0)
        lax.fori_loop(0, n_tiles, zwait, 0)

    def row_copy(src_row, dst_row):
        return pltpu.make_async_copy(h_ref.at[pl.ds(src_row, 1)], xs_hbm.at[pl.ds(dst_row, 1)], sem)

    def start_body(t, carry):
        for k in range(TOP_K):
            row_copy(t, dest_ref[0, k, t]).start()
        return carry

    def wait_body(t, carry):
        for k in range(TOP_K):
            row_copy(0, 0).wait()
        return carry

    lax.fori_loop(0, tt, start_body, 0)
    lax.fori_loop(0, tt, wait_body, 0)


def _moe_gather(zflag, h2, dest, n_rows, *, tt, tm):
    t, d = h2.shape
    dest3 = dest.reshape(TOP_K, t // tt, tt).transpose(1, 0, 2)
    kern = functools.partial(_gather_kernel, tt=tt, tm=tm)
    return pl.pallas_call(
        kern,
        grid_spec=pltpu.PrefetchScalarGridSpec(
            num_scalar_prefetch=1,
            grid=(t // tt,),
            in_specs=[pl.BlockSpec((1, TOP_K, tt), lambda i, zf: (i, 0, 0), memory_space=pltpu.SMEM),
                      pl.BlockSpec((tt, d), lambda i, zf: (i, 0))],
            out_specs=pl.BlockSpec(memory_space=pl.ANY),
            scratch_shapes=[pltpu.VMEM((tm, d), h2.dtype), pltpu.SemaphoreType.DMA(()),
                            pltpu.SemaphoreType.DMA(())],
        ),
        out_shape=jax.ShapeDtypeStruct((n_rows, d), h2.dtype),
        compiler_params=_cparams("arbitrary"),
        name="moe_gather",
    )(zflag, dest3, h2)


def _expert_row_tiles(ts_ref, nt_ref, nv_ref, src_hbm, dst_hbm, in_buf, out_buf, zero_ref, in_sem, out_sem, zsem,
                      compute):
    j, e = pl.program_id(0), pl.program_id(1)
    nj, ne = pl.num_programs(0), pl.num_programs(1)
    tm, tn = out_buf.shape[1], out_buf.shape[2]
    n_tiles = dst_hbm.shape[0] // tm
    nv = nv_ref[0]
    t0 = ts_ref[e]
    col = pl.multiple_of(j * tn, tn)

    def rows(g):
        return pl.ds(pl.multiple_of(g * tm, tm), tm)

    def in_copy(g, slot):
        return pltpu.make_async_copy(src_hbm.at[rows(g)], in_buf.at[slot], in_sem.at[slot])

    def out_copy(g, slot):
        return pltpu.make_async_copy(out_buf.at[slot], dst_hbm.at[rows(g), pl.ds(col, tn)], out_sem.at[slot])

    def zero_copy(g):
        return pltpu.make_async_copy(zero_ref, dst_hbm.at[rows(g), pl.ds(col, tn)], zsem)

    @pl.when(jnp.logical_and(j == 0, e == 0))
    def _():
        zero_ref[...] = jnp.zeros_like(zero_ref)

    def tile_body(t, carry):
        g = t0 + t
        seq = j * nv + g
        slot = seq % 2

        @pl.when(seq == 0)
        def _():
            in_copy(g, slot).start()

        in_copy(g, slot).wait()
        last_of_sweep = g + 1 == nv

        @pl.when(jnp.logical_or(jnp.logical_not(last_of_sweep), j + 1 < nj))
        def _():
            in_copy(jnp.where(last_of_sweep, 0, g + 1), 1 - slot).start()

        y = compute(in_buf[slot])

        @pl.when(seq >= 2)
        def _():
            out_copy(g, slot).wait()

        out_buf[slot] = y
        out_copy(g, slot).start()
        return carry

    lax.fori_loop(0, nt_ref[e], tile_body, 0)

    @pl.when(e == ne - 1)
    def _():
        def zstart(g, carry):
            zero_copy(g).start()
            return carry

        def zwait(g, carry):
            zero_copy(g).wait()
            return carry

        lax.fori_loop(nv, n_tiles, zstart, 0)
        lax.fori_loop(nv, n_tiles, zwait, 0)

    @pl.when(jnp.logical_and(e == ne - 1, j == nj - 1))
    def _():
        total = nj * nv

        @pl.when(total >= 1)
        def _():
            out_copy(0, (total - 1) % 2).wait()

        @pl.when(total >= 2)
        def _():
            out_copy(0, total % 2).wait()


def _expert_up_kernel(ts_ref, nt_ref, nv_ref, wg_ref, bg_ref, wu_ref, bu_ref, xs_hbm, act_hbm,
                      wgc_ref, wuc_ref, in_buf, out_buf, zero_ref, in_sem, out_sem, zsem):
    @pl.when(nt_ref[pl.program_id(1)] > 0)
    def _():
        wgc_ref[...] = wg_ref[0].astype(BF16)
        wuc_ref[...] = wu_ref[0].astype(BF16)

    def compute(xw):
        half = xw.shape[1]
        x_lo = lax.bitcast_convert_type(xw << 16, F32).astype(BF16)
        x_hi = lax.bitcast_convert_type((xw >> 16) << 16, F32).astype(BF16)

        def proj(w_ref, b_ref):
            y = jnp.dot(x_lo, w_ref[:half, :], preferred_element_type=F32)
            return y + jnp.dot(x_hi, w_ref[half:, :], preferred_element_type=F32) + b_ref[0]

        gate = proj(wgc_ref, bg_ref)
        lin = proj(wuc_ref, bu_ref)
        gate = jnp.minimum(gate, SWIGLU_LIMIT)
        lin = jnp.clip(lin, -SWIGLU_LIMIT, SWIGLU_LIMIT)
        return (gate * jax.nn.sigmoid(SWIGLU_ALPHA * gate) * (lin + 1.0)).astype(BF16)

    _expert_row_tiles(ts_ref, nt_ref, nv_ref, xs_hbm, act_hbm, in_buf, out_buf, zero_ref, in_sem, out_sem, zsem,
                      compute)


def _w_blk(j, e, ts, nt, nv):
    return (e, 0, j)


def _row_tile_scratch(tm, k_in, in_dtype, tn, out_dtype):
    return [pltpu.VMEM((2, tm, k_in), in_dtype), pltpu.VMEM((2, tm, tn), out_dtype), pltpu.VMEM((tm, tn), out_dtype),
            pltpu.SemaphoreType.DMA((2,)), pltpu.SemaphoreType.DMA((2,)), pltpu.SemaphoreType.DMA(())]


def _expert_up(tile_start, tiles_per_e, n_valid, xs, wg, bg, wu, bu, *, tm, tn):
    r, dw = xs.shape
    n_e, d, de = wg.shape
    assert 2 * dw == d, "xs rows hold two bf16 columns per 32-bit word"
    return pl.pallas_call(
        _expert_up_kernel,
        grid_spec=pltpu.PrefetchScalarGridSpec(
            num_scalar_prefetch=3,
            grid=(de // tn, n_e),
            in_specs=[pl.BlockSpec((1, d, tn), _w_blk),
                      pl.BlockSpec((1, 1, tn), _w_blk),
                      pl.BlockSpec((1, d, tn), _w_blk),
                      pl.BlockSpec((1, 1, tn), _w_blk),
                      pl.BlockSpec(memory_space=pl.ANY)],
            out_specs=pl.BlockSpec(memory_space=pl.ANY),
            scratch_shapes=[pltpu.VMEM((d, tn), BF16), pltpu.VMEM((d, tn), BF16)]
            + _row_tile_scratch(tm, dw, xs.dtype, tn, BF16),
        ),
        out_shape=jax.ShapeDtypeStruct((r, de), BF16),
        compiler_params=_cparams("arbitrary", "arbitrary"),
        name="expert_up",
    )(tile_start, tiles_per_e, n_valid, wg, bg, wu, bu, xs)


def _expert_down_kernel(ts_ref, nt_ref, nv_ref, w_ref, b_ref, act_hbm, ys_hbm,
                        wc_ref, in_buf, out_buf, zero_ref, in_sem, out_sem, zsem):
    @pl.when(nt_ref[pl.program_id(1)] > 0)
    def _():
        wc_ref[...] = w_ref[0].astype(BF16)

    def compute(a):
        return jnp.dot(a, wc_ref[...], preferred_element_type=F32) + b_ref[0]

    _expert_row_tiles(ts_ref, nt_ref, nv_ref, act_hbm, ys_hbm, in_buf, out_buf, zero_ref, in_sem, out_sem, zsem,
                      compute)


def _expert_down(tile_start, tiles_per_e, n_valid, act, wd, bd, *, tm, tn):
    r, de = act.shape
    n_e, _, d = wd.shape
    return pl.pallas_call(
        _expert_down_kernel,
        grid_spec=pltpu.PrefetchScalarGridSpec(
            num_scalar_prefetch=3,
            grid=(d // tn, n_e),
            in_specs=[pl.BlockSpec((1, de, tn), _w_blk),
                      pl.BlockSpec((1, 1, tn), _w_blk),
                      pl.BlockSpec(memory_space=pl.ANY)],
            out_specs=pl.BlockSpec(memory_space=pl.ANY),
            scratch_shapes=[pltpu.VMEM((de, tn), BF16)] + _row_tile_scratch(tm, de, act.dtype, tn, F32),
        ),
        out_shape=jax.ShapeDtypeStruct((r, d), F32),
        compiler_params=_cparams("arbitrary", "arbitrary"),
        name="expert_down",
    )(tile_start, tiles_per_e, n_valid, wd, bd, act)


def _combine_kernel(dest_ref, x_ref, g_ref, ys_hbm, o_ref, buf_ref, sem, *, tt):
    def row_copy(src_row, k, t):
        return pltpu.make_async_copy(ys_hbm.at[pl.ds(src_row, 1)], buf_ref.at[k, pl.ds(t, 1)], sem)

    def start_body(t, carry):
        for k in range(TOP_K):
            row_copy(dest_ref[0, k, t], k, t).start()
        return carry

    def wait_body(t, carry):
        for k in range(TOP_K):
            row_copy(0, k, t).wait()
        return carry

    lax.fori_loop(0, tt, start_body, 0)
    lax.fori_loop(0, tt, wait_body, 0)
    g = g_ref[...]
    acc = x_ref[...]
    for k in range(TOP_K):
        acc = acc + g[:, k:k + 1] * buf_ref[k]
    o_ref[...] = acc


def _moe_combine(dest, x1, gates_t, ys, *, tt):
    t, d = x1.shape
    dest3 = dest.reshape(TOP_K, t // tt, tt).transpose(1, 0, 2)
    kern = functools.partial(_combine_kernel, tt=tt)
    return pl.pallas_call(
        kern,
        grid=(t // tt,),
        in_specs=[pl.BlockSpec((1, TOP_K, tt), lambda i: (i, 0, 0), memory_space=pltpu.SMEM),
                  pl.BlockSpec((tt, d), lambda i: (i, 0)),
                  pl.BlockSpec((tt, TOP_K), lambda i: (i, 0)),
                  pl.BlockSpec(memory_space=pl.ANY)],
        out_specs=pl.BlockSpec((tt, d), lambda i: (i, 0)),
        out_shape=jax.ShapeDtypeStruct((t, d), F32),
        scratch_shapes=[pltpu.VMEM((TOP_K, tt, d), F32), pltpu.SemaphoreType.DMA(())],
        compiler_params=_cparams("arbitrary"),
        name="moe_combine",
    )(dest3, x1, gates_t, ys)


def _rope_tables(positions):
    half = QK_ROPE_DIM // 2
    inv_freq = ROPE_THETA ** (-jnp.arange(0, QK_ROPE_DIM, 2, dtype=F32) / QK_ROPE_DIM)
    ang = positions.astype(F32)[:, None] * inv_freq
    cos, sin = jnp.cos(ang), jnp.sin(ang)
    zeros = jnp.zeros((positions.shape[0], LANES - 2 * half), F32)
    return jnp.concatenate([cos, cos, zeros], axis=1), jnp.concatenate([-sin, sin, zeros], axis=1)


def _swap_halves(w):
    half = w.shape[-1] // 2
    return jnp.concatenate([w[..., half:], w[..., :half]], axis=-1)


def _layer(x, positions, attn_norm_w, w_in, q_a_norm_w, w_uq, kv_a_norm_w, w_ukv, q_head_norm_w, k_head_norm_w,
           conv_dw_w, conv_dw_b, conv_ln_w, conv_ln_b, attn_out_norm_w, conv_out_norm_w, w_o, ffn_norm_w,
           w_router, b_router, w_gate, b_gate, w_up, b_up, w_down, b_down):
    s, d = x.shape
    q_rank = q_a_norm_w.shape[0]
    kv_rank = kv_a_norm_w.shape[0]
    n_conv = conv_dw_b.shape[0]
    glu_w = 2 * n_conv
    de = w_gate.shape[2]

    pad64 = jnp.zeros((d, LANES - QK_ROPE_DIM), F32)
    w_kpe = w_in[:, glu_w + q_rank + kv_rank:]
    z_cols = glu_w + q_rank + kv_rank + 2 * LANES
    in_tn = 1024
    z_pad = -z_cols % in_tn
    w_in_p = jnp.concatenate(
        [w_in[:, :glu_w + q_rank + kv_rank], w_kpe, pad64, _swap_halves(w_kpe), pad64, jnp.zeros((d, z_pad), F32)],
        axis=1).astype(BF16)

    wq = w_uq.reshape(q_rank, N_HEADS, QK_HEAD_DIM)
    wq_pe = wq[:, :, QK_NOPE_DIM:]
    zq = jnp.zeros((q_rank, N_HEADS, LANES - QK_ROPE_DIM), F32)
    wq3 = jnp.concatenate([wq[:, :, :QK_NOPE_DIM], wq_pe, zq, _swap_halves(wq_pe), zq], axis=2)
    wq3 = wq3.transpose(1, 0, 2).astype(BF16)
    wkv3 = w_ukv.reshape(kv_rank, N_HEADS, QK_NOPE_DIM + V_HEAD_DIM).transpose(1, 0, 2).astype(BF16)
    zero_hw = jnp.zeros((QK_PAD_DIM - QK_HEAD_DIM,), F32)
    q_hw = jnp.concatenate([q_head_norm_w, zero_hw])[None]
    k_hw = jnp.concatenate([k_head_norm_w, zero_hw])[None]
    cos_t, sin_t = _rope_tables(positions)

    z = _norm_matmul(x, attn_norm_w[None], w_in_p, tm=512, tn=in_tn)
    conv_n = _conv_group(z, conv_dw_w, conv_dw_b, conv_ln_w, conv_ln_b, conv_out_norm_w, n_ch=n_conv, ts=256)
    q_scale = QK_HEAD_DIM ** -0.5 * LOG2E
    q = _q_proj(z, q_a_norm_w[None], wq3, cos_t, sin_t, q_hw, col_blk=glu_w // q_rank, tm=512, hb=4,
                q_scale=q_scale)
    k, v = _kv_proj(z, kv_a_norm_w[None], wkv3, cos_t, sin_t, k_hw, ckv_blk=(glu_w + q_rank) // kv_rank,
                    kpe_blk=(glu_w + q_rank + kv_rank) // LANES, tm=512, hb=4)
    attn = _attention(q, k, v, tq=512, tk=512)
    x1 = _out_proj(attn, conv_n, x, attn_out_norm_w[None], w_o.astype(BF16), tm=512, tn=1024)

    tm_e = 256
    h2, top_idx, gates, rank, counts = _router(x1, ffn_norm_w[None], w_router.T, b_router[:, None], tm=512)
    counts = counts[:, 0]
    tiles_per_e = (counts + tm_e - 1) // tm_e
    tile_end = jnp.cumsum(tiles_per_e)
    tile_start = tile_end - tiles_per_e
    n_valid = tile_end[-1:]
    expert_ids = jnp.arange(N_EXPERTS, dtype=I32)[:, None, None]
    row_start = (tile_start * tm_e)[:, None, None]
    dest = jnp.sum(jnp.where(top_idx[None] == expert_ids, row_start, 0), axis=0) + rank
    n_rows = s * TOP_K + N_EXPERTS * tm_e
    tile_ids = jnp.arange(n_rows // tm_e, dtype=I32)
    has_pad_rows = jnp.any((tile_ids[:, None] == tile_end[None, :] - 1) & (counts % tm_e != 0)[None, :], axis=1)
    zflag = (has_pad_rows | (tile_ids >= n_valid[0])).astype(I32)

    xs = _moe_gather(zflag, h2, dest, n_rows, tt=512, tm=tm_e)
    act = _expert_up(tile_start, tiles_per_e, n_valid, xs, w_gate, b_gate[:, None, :], w_up, b_up[:, None, :],
                     tm=tm_e, tn=512)
    ys = _expert_down(tile_start, tiles_per_e, n_valid, act, w_down, b_down[:, None, :], tm=tm_e, tn=2048)
    return _moe_combine(dest, x1, gates.T, ys, tt=128)


def kernel(x, positions, attn_norm_w, w_in, q_a_norm_w, w_uq, kv_a_norm_w, w_ukv, q_head_norm_w, k_head_norm_w,
           conv_dw_w, conv_dw_b, conv_ln_w, conv_ln_b, attn_out_norm_w, conv_out_norm_w, w_o, ffn_norm_w,
           w_router, b_router, w_gate, b_gate, w_up, b_up, w_down, b_down):
    b, s, d = x.shape
    depth = attn_norm_w.shape[0]
    assert b == 1, "one sequence per call"
    xs = x.reshape(s, d)
    for l in range(depth):
        xs = _layer(xs, positions[0], attn_norm_w[l], w_in[l], q_a_norm_w[l], w_uq[l], kv_a_norm_w[l], w_ukv[l],
                    q_head_norm_w[l], k_head_norm_w[l], conv_dw_w[l][:, 0, :], conv_dw_b[l], conv_ln_w[l],
                    conv_ln_b[l], attn_out_norm_w[l], conv_out_norm_w[l], w_o[l], ffn_norm_w[l], w_router[l],
                    b_router[l], w_gate[l], b_gate[l], w_up[l], b_up[l], w_down[l], b_down[l])
    return xs.reshape(b, s, d)
```

```python
import functools

import jax
import jax.numpy as jnp
from jax import lax
from jax.experimental import pallas as pl
from jax.experimental.pallas import tpu as pltpu

F32, BF16, I32, U32 = jnp.float32, jnp.bfloat16, jnp.int32, jnp.uint32

N_HEADS = 16
QK_NOPE_DIM = 128
QK_ROPE_DIM = 64
QK_HEAD_DIM = QK_NOPE_DIM + QK_ROPE_DIM
V_HEAD_DIM = 128
QK_PAD_DIM = 256
ROPE_THETA = 10000.0
CONV_WIDTH = 31
N_EXPERTS = 32
TOP_K = 4
SWIGLU_LIMIT = 7.0
SWIGLU_ALPHA = 1.702
EPS = 1e-6
LOG2E = 1.4426950408889634

LANES = 128
CONV_HALO = 16
VMEM_LIMIT_BYTES = 56 * 1024 * 1024


def _cparams(*sem):
    return pltpu.CompilerParams(dimension_semantics=sem, vmem_limit_bytes=VMEM_LIMIT_BYTES)


def _rms_scale(x, n):
    return lax.rsqrt(jnp.sum(x * x, axis=-1, keepdims=True) * (1.0 / n) + EPS)


def _norm_matmul_kernel(x_ref, nw_ref, w_ref, o_ref, h_ref):
    @pl.when(pl.program_id(1) == 0)
    def _():
        x = x_ref[...]
        h_ref[...] = (x * _rms_scale(x, x.shape[-1]) * nw_ref[...]).astype(BF16)

    o_ref[...] = jnp.dot(h_ref[...], w_ref[...], preferred_element_type=F32)


def _norm_matmul(x, nw, w, *, tm, tn):
    m, k = x.shape
    n = w.shape[1]
    return pl.pallas_call(
        _norm_matmul_kernel,
        grid=(m // tm, n // tn),
        in_specs=[pl.BlockSpec((tm, k), lambda i, j: (i, 0)),
                  pl.BlockSpec((1, k), lambda i, j: (0, 0)),
                  pl.BlockSpec((k, tn), lambda i, j: (0, j))],
        out_specs=pl.BlockSpec((tm, tn), lambda i, j: (i, j)),
        out_shape=jax.ShapeDtypeStruct((m, n), F32),
        scratch_shapes=[pltpu.VMEM((tm, k), BF16)],
        compiler_params=_cparams("parallel", "arbitrary"),
        name="in_proj",
    )(x, nw, w)


def _conv_kernel(a_ref, g_ref, ap_ref, gp_ref, an_ref, gn_ref, w_ref, b_ref, lnw_ref, lnb_ref, onw_ref,
                 o_ref, uext_ref, conv_ref, *, ts, nc, rb):
    i = pl.program_id(0)
    has_prev = i > 0
    has_next = i < pl.num_programs(0) - 1
    halo = CONV_HALO

    def glu(a, g):
        return a * jax.nn.sigmoid(g)

    for c in range(nc):
        sl = slice(c * LANES, (c + 1) * LANES)
        uext_ref[c, 0:halo, :] = jnp.where(has_prev, glu(ap_ref[:, sl], gp_ref[:, sl]), 0.0)
        uext_ref[c, halo:halo + ts, :] = glu(a_ref[:, sl], g_ref[:, sl])
        uext_ref[c, halo + ts:halo + ts + halo, :] = jnp.where(has_next, glu(an_ref[:, sl], gn_ref[:, sl]), 0.0)

    off0 = halo - CONV_WIDTH // 2

    def chunk_body(c, carry):
        w = w_ref[c]
        wk = [jnp.broadcast_to(w[k:k + 1, :], (rb, LANES)) for k in range(CONV_WIDTH)]
        bias = jnp.broadcast_to(b_ref[c], (rb, LANES))
        for r in range(ts // rb):
            acc = bias
            for k in range(CONV_WIDTH):
                acc = acc + wk[k] * uext_ref[c, r * rb + off0 + k:r * rb + off0 + k + rb, :]
            conv_ref[c, r * rb:(r + 1) * rb, :] = acc
        return carry

    lax.fori_loop(0, nc, chunk_body, 0)

    n_ch = nc * LANES
    xc = conv_ref[...]
    mu = jnp.sum(jnp.sum(xc, axis=0), axis=-1, keepdims=True) * (1.0 / n_ch)
    d = xc - mu[None]
    var = jnp.sum(jnp.sum(d * d, axis=0), axis=-1, keepdims=True) * (1.0 / n_ch)
    y = d * lax.rsqrt(var + EPS)[None] * lnw_ref[...] + lnb_ref[...]
    y = y * jax.nn.sigmoid(y)
    ms = jnp.sum(jnp.sum(y * y, axis=0), axis=-1, keepdims=True) * (1.0 / n_ch)
    o = y * lax.rsqrt(ms + EPS)[None] * onw_ref[...]
    for c in range(nc):
        o_ref[:, c * LANES:(c + 1) * LANES] = o[c].astype(BF16)


def _conv_group(z, dw_w, dw_b, ln_w, ln_b, on_w, *, n_ch, ts):
    s = z.shape[0]
    nc = n_ch // LANES
    hb = ts // CONV_HALO
    n_hblk = s // CONV_HALO

    def chunked(v):
        return v.reshape(nc, 1, LANES)

    w = dw_w.reshape(CONV_WIDTH, nc, LANES).transpose(1, 0, 2)
    w = jnp.pad(w, ((0, 0), (0, 32 - CONV_WIDTH), (0, 0)))
    vec_spec = pl.BlockSpec((nc, 1, LANES), lambda i: (0, 0, 0))
    kern = functools.partial(_conv_kernel, ts=ts, nc=nc, rb=64)
    return pl.pallas_call(
        kern,
        grid=(s // ts,),
        in_specs=[pl.BlockSpec((ts, n_ch), lambda i: (i, 0)),
                  pl.BlockSpec((ts, n_ch), lambda i: (i, 1)),
                  pl.BlockSpec((CONV_HALO, n_ch), lambda i: (jnp.maximum(i * hb - 1, 0), 0)),
                  pl.BlockSpec((CONV_HALO, n_ch), lambda i: (jnp.maximum(i * hb - 1, 0), 1)),
                  pl.BlockSpec((CONV_HALO, n_ch), lambda i: (jnp.minimum((i + 1) * hb, n_hblk - 1), 0)),
                  pl.BlockSpec((CONV_HALO, n_ch), lambda i: (jnp.minimum((i + 1) * hb, n_hblk - 1), 1)),
                  pl.BlockSpec((nc, 32, LANES), lambda i: (0, 0, 0)),
                  vec_spec, vec_spec, vec_spec, vec_spec],
        out_specs=pl.BlockSpec((ts, n_ch), lambda i: (i, 0)),
        out_shape=jax.ShapeDtypeStruct((s, n_ch), BF16),
        scratch_shapes=[pltpu.VMEM((nc, ts + 2 * CONV_HALO, LANES), F32),
                        pltpu.VMEM((nc, ts, LANES), F32)],
        compiler_params=_cparams("parallel"),
        name="conv_group",
    )(z, z, z, z, z, z, w, chunked(dw_b), chunked(ln_w), chunked(ln_b), chunked(on_w))


def _q_kernel(z_ref, nw_ref, w_ref, cos_ref, sin_ref, hw_ref, q_ref, zn_ref, *, q_scale):
    @pl.when(pl.program_id(1) == 0)
    def _():
        z = z_ref[...]
        zn_ref[...] = (z * _rms_scale(z, z.shape[-1]) * nw_ref[...]).astype(BF16)

    hw = hw_ref[...]
    for hh in range(w_ref.shape[0]):
        r = jnp.dot(zn_ref[...], w_ref[hh], preferred_element_type=F32)
        nope = r[:, :LANES]
        rope = r[:, LANES:2 * LANES] * cos_ref[...] + r[:, 2 * LANES:] * sin_ref[...]
        ss = jnp.sum(nope * nope, axis=-1, keepdims=True) + jnp.sum(rope * rope, axis=-1, keepdims=True)
        inv = lax.rsqrt(ss * (1.0 / QK_HEAD_DIM) + EPS) * q_scale
        q_ref[hh, :LANES, :] = (nope * inv * hw[:, :LANES]).T.astype(BF16)
        q_ref[hh, LANES:, :] = (rope * inv * hw[:, LANES:]).T.astype(BF16)


def _q_proj(z, nw, w3, cos_t, sin_t, hw, *, col_blk, tm, hb, q_scale):
    s = z.shape[0]
    rank = w3.shape[1]
    kern = functools.partial(_q_kernel, q_scale=q_scale)
    return pl.pallas_call(
        kern,
        grid=(s // tm, N_HEADS // hb),
        in_specs=[pl.BlockSpec((tm, rank), lambda i, h: (i, col_blk)),
                  pl.BlockSpec((1, rank), lambda i, h: (0, 0)),
                  pl.BlockSpec((hb, rank, 3 * LANES), lambda i, h: (h, 0, 0)),
                  pl.BlockSpec((tm, LANES), lambda i, h: (i, 0)),
                  pl.BlockSpec((tm, LANES), lambda i, h: (i, 0)),
                  pl.BlockSpec((1, QK_PAD_DIM), lambda i, h: (0, 0))],
        out_specs=pl.BlockSpec((hb, QK_PAD_DIM, tm), lambda i, h: (h, 0, i)),
        out_shape=jax.ShapeDtypeStruct((N_HEADS, QK_PAD_DIM, s), BF16),
        scratch_shapes=[pltpu.VMEM((tm, rank), BF16)],
        compiler_params=_cparams("parallel", "arbitrary"),
        name="q_proj",
    )(z, nw, w3, cos_t, sin_t, hw)


def _kv_kernel(z_ref, kpe_ref, kpes_ref, nw_ref, w_ref, cos_ref, sin_ref, hw_ref, k_ref, v_ref,
               zn_ref, kr_ref, ss_ref):
    @pl.when(pl.program_id(1) == 0)
    def _():
        z = z_ref[...]
        zn_ref[...] = (z * _rms_scale(z, z.shape[-1]) * nw_ref[...]).astype(BF16)
        kr = kpe_ref[...] * cos_ref[...] + kpes_ref[...] * sin_ref[...]
        kr_ref[...] = kr
        ss_ref[...] = jnp.sum(kr * kr, axis=-1, keepdims=True)

    hw = hw_ref[...]
    for hh in range(w_ref.shape[0]):
        r = jnp.dot(zn_ref[...], w_ref[hh], preferred_element_type=F32)
        kn = r[:, :LANES]
        ss = jnp.sum(kn * kn, axis=-1, keepdims=True) + ss_ref[...]
        inv = lax.rsqrt(ss * (1.0 / QK_HEAD_DIM) + EPS)
        k_ref[hh, :, :LANES] = (kn * inv * hw[:, :LANES]).astype(BF16)
        k_ref[hh, :, LANES:] = (kr_ref[...] * inv * hw[:, LANES:]).astype(BF16)
        v_ref[hh] = r[:, LANES:].T.astype(BF16)


def _kv_proj(z, nw, w3, cos_t, sin_t, hw, *, ckv_blk, kpe_blk, tm, hb):
    s = z.shape[0]
    rank = w3.shape[1]
    return pl.pallas_call(
        _kv_kernel,
        grid=(s // tm, N_HEADS // hb),
        in_specs=[pl.BlockSpec((tm, rank), lambda i, h: (i, ckv_blk)),
                  pl.BlockSpec((tm, LANES), lambda i, h: (i, kpe_blk)),
                  pl.BlockSpec((tm, LANES), lambda i, h: (i, kpe_blk + 1)),
                  pl.BlockSpec((1, rank), lambda i, h: (0, 0)),
                  pl.BlockSpec((hb, rank, 2 * LANES), lambda i, h: (h, 0, 0)),
                  pl.BlockSpec((tm, LANES), lambda i, h: (i, 0)),
                  pl.BlockSpec((tm, LANES), lambda i, h: (i, 0)),
                  pl.BlockSpec((1, QK_PAD_DIM), lambda i, h: (0, 0))],
        out_specs=[pl.BlockSpec((hb, tm, QK_PAD_DIM), lambda i, h: (h, i, 0)),
                   pl.BlockSpec((hb, V_HEAD_DIM, tm), lambda i, h: (h, 0, i))],
        out_shape=[jax.ShapeDtypeStruct((N_HEADS, s, QK_PAD_DIM), BF16),
                   jax.ShapeDtypeStruct((N_HEADS, V_HEAD_DIM, s), BF16)],
        scratch_shapes=[pltpu.VMEM((tm, rank), BF16),
                        pltpu.VMEM((tm, LANES), F32),
                        pltpu.VMEM((tm, 1), F32)],
        compiler_params=_cparams("parallel", "arbitrary"),
        name="kv_proj",
    )(z, z, z, nw, w3, cos_t, sin_t, hw)


def _attn_kernel(qt_ref, k_ref, vt_ref, o_ref, st_ref, *, tk):
    qt = qt_ref[0]
    tq = qt.shape[1]
    nk = k_ref.shape[1] // tk

    def scores(j, slot):
        start = pl.multiple_of(j * tk, tk)
        st_ref[slot] = jnp.dot(k_ref[0, pl.ds(start, tk), :], qt, preferred_element_type=F32)

    def update(j, slot, carry):
        m, l, acc = carry
        start = pl.multiple_of(j * tk, tk)
        vts = vt_ref[0, :, pl.ds(start, tk)]
        st = st_ref[slot]
        m_new = jnp.maximum(m, jnp.max(st, axis=0, keepdims=True))
        p = jnp.exp2(st - m_new)
        alpha = jnp.exp2(m - m_new)
        l = alpha * l + jnp.sum(p, axis=0, keepdims=True)
        acc = alpha * acc + jnp.dot(vts, p.astype(BF16), preferred_element_type=F32)
        return m_new, l, acc

    def body(jj, carry):
        j = 2 * jj
        scores(j + 1, 1)
        carry = update(j, 0, carry)
        scores(jnp.minimum(j + 2, nk - 1), 0)
        return update(j + 1, 1, carry)

    m0 = jnp.full((1, tq), -jnp.inf, F32)
    l0 = jnp.zeros((1, tq), F32)
    acc0 = jnp.zeros((V_HEAD_DIM, tq), F32)
    scores(0, 0)
    _, l, acc = lax.fori_loop(0, nk // 2, body, (m0, l0, acc0), unroll=2)
    o_ref[...] = (acc / l).T


def _attention(qt, k, vt, *, tq, tk):
    h, s, _ = k.shape
    kern = functools.partial(_attn_kernel, tk=tk)
    return pl.pallas_call(
        kern,
        grid=(h, s // tq),
        in_specs=[pl.BlockSpec((1, QK_PAD_DIM, tq), lambda hh, i: (hh, 0, i)),
                  pl.BlockSpec((1, s, QK_PAD_DIM), lambda hh, i: (hh, 0, 0)),
                  pl.BlockSpec((1, V_HEAD_DIM, s), lambda hh, i: (hh, 0, 0))],
        out_specs=pl.BlockSpec((tq, V_HEAD_DIM), lambda hh, i: (i, hh)),
        out_shape=jax.ShapeDtypeStruct((s, h * V_HEAD_DIM), F32),
        scratch_shapes=[pltpu.VMEM((2, tk, tq), F32)],
        compiler_params=_cparams("parallel", "arbitrary"),
        name="attention",
    )(qt, k, vt)


def _out_proj_kernel(a_ref, c_ref, x_ref, nw_ref, wa_ref, wc_ref, o_ref, an_ref):
    @pl.when(pl.program_id(1) == 0)
    def _():
        a = a_ref[...]
        an_ref[...] = (a * _rms_scale(a, a.shape[-1]) * nw_ref[...]).astype(BF16)

    acc = jnp.dot(an_ref[...], wa_ref[...], preferred_element_type=F32)
    acc = acc + jnp.dot(c_ref[...], wc_ref[...], preferred_element_type=F32)
    o_ref[...] = x_ref[...] + acc


def _out_proj(attn, conv_n, x, nw, w_o, *, tm, tn):
    s, da = attn.shape
    dc = conv_n.shape[1]
    d = x.shape[1]
    assert da == dc, "the two head groups share one W_o row-block size"
    return pl.pallas_call(
        _out_proj_kernel,
        grid=(s // tm, d // tn),
        in_specs=[pl.BlockSpec((tm, da), lambda i, j: (i, 0)),
                  pl.BlockSpec((tm, dc), lambda i, j: (i, 0)),
                  pl.BlockSpec((tm, tn), lambda i, j: (i, j)),
                  pl.BlockSpec((1, da), lambda i, j: (0, 0)),
                  pl.BlockSpec((da, tn), lambda i, j: (0, j)),
                  pl.BlockSpec((dc, tn), lambda i, j: (1, j))],
        out_specs=pl.BlockSpec((tm, tn), lambda i, j: (i, j)),
        out_shape=jax.ShapeDtypeStruct((s, d), F32),
        scratch_shapes=[pltpu.VMEM((tm, da), BF16)],
        compiler_params=_cparams("parallel", "arbitrary"),
        name="out_proj",
    )(attn, conv_n, x, nw, w_o, w_o)


def _router_kernel(x_ref, nw_ref, wr_ref, br_ref, tri_ref, h_ref, idx_ref, gate_ref, rank_ref, cnt_ref, run_ref):
    @pl.when(pl.program_id(0) == 0)
    def _():
        run_ref[...] = jnp.zeros_like(run_ref)

    x = x_ref[...]
    h = x * _rms_scale(x, x.shape[-1]) * nw_ref[...]
    half = h.shape[1] // 2
    lo = lax.bitcast_convert_type(h[:, :half].astype(BF16).astype(F32), U32)
    hi = lax.bitcast_convert_type(h[:, half:].astype(BF16).astype(F32), U32)
    h_ref[...] = (lo >> 16) | hi
    tm = x.shape[0]
    logits = lax.dot_general(wr_ref[...], h, (((1,), (1,)), ((), ())), preferred_element_type=F32,
                             precision=lax.Precision.HIGHEST) + br_ref[...]
    eio = lax.broadcasted_iota(I32, (N_EXPERTS, tm), 0)
    vals, idxs = [], []
    cur = logits
    for _ in range(TOP_K):
        m = jnp.max(cur, axis=0, keepdims=True)
        idx = jnp.min(jnp.where(cur == m, eio, N_EXPERTS), axis=0, keepdims=True)
        vals.append(m)
        idxs.append(idx)
        cur = jnp.where(eio == idx, -jnp.inf, cur)
    es = [jnp.exp(v - vals[0]) for v in vals]
    den = es[0] + es[1] + es[2] + es[3]
    base = run_ref[...]
    for k in range(TOP_K):
        gate_ref[k:k + 1, :] = es[k] / den
        idx_ref[k:k + 1, :] = idxs[k]
        onehot = eio == idxs[k]
        before = jnp.dot(onehot.astype(BF16), tri_ref[...], preferred_element_type=F32)
        rank = jnp.sum(jnp.where(onehot, before + base, 0.0), axis=0, keepdims=True)
        rank_ref[k:k + 1, :] = rank.astype(I32)
        base = base + jnp.sum(onehot.astype(F32), axis=1, keepdims=True)
    run_ref[...] = base
    cnt_ref[...] = base.astype(I32)


def _router(x1, nw, wr_t, br, *, tm):
    t, d = x1.shape
    tri = (lax.broadcasted_iota(I32, (tm, tm), 0) < lax.broadcasted_iota(I32, (tm, tm), 1)).astype(BF16)
    row4 = pl.BlockSpec((TOP_K, tm), lambda i: (0, i))
    return pl.pallas_call(
        _router_kernel,
        grid=(t // tm,),
        in_specs=[pl.BlockSpec((tm, d), lambda i: (i, 0)),
                  pl.BlockSpec((1, d), lambda i: (0, 0)),
                  pl.BlockSpec((N_EXPERTS, d), lambda i: (0, 0)),
                  pl.BlockSpec((N_EXPERTS, 1), lambda i: (0, 0)),
                  pl.BlockSpec((tm, tm), lambda i: (0, 0))],
        out_specs=[pl.BlockSpec((tm, d // 2), lambda i: (i, 0)), row4, row4, row4,
                   pl.BlockSpec((N_EXPERTS, 1), lambda i: (0, 0))],
        out_shape=[jax.ShapeDtypeStruct((t, d // 2), U32),
                   jax.ShapeDtypeStruct((TOP_K, t), I32),
                   jax.ShapeDtypeStruct((TOP_K, t), F32),
                   jax.ShapeDtypeStruct((TOP_K, t), I32),
                   jax.ShapeDtypeStruct((N_EXPERTS, 1), I32)],
        scratch_shapes=[pltpu.VMEM((N_EXPERTS, 1), F32)],
        compiler_params=_cparams("arbitrary"),
        name="router",
    )(x1, nw, wr_t, br, tri)


def _gather_kernel(zflag_ref, dest_ref, h_ref, xs_hbm, zero_ref, sem, zsem, *, tt, tm):
    n_tiles = xs_hbm.shape[0] // tm

    @pl.when(pl.program_id(0) == 0)
    def _():
        zero_ref[...] = jnp.zeros_like(zero_ref)

        def zero_copy(g):
            return pltpu.make_async_copy(zero_ref, xs_hbm.at[pl.ds(pl.multiple_of(g * tm, tm), tm)], zsem)

        def zstart(g, carry):
            @pl.when(zflag_ref[g] != 0)
            def _():
                zero_copy(g).start()
            return carry

        def zwait(g, carry):
            @pl.when(zflag_ref[g] != 0)
            def _():
                zero_copy(g).wait()
            return carry

        lax.fori_loop(0, n_tiles, zstart, 0)
        lax.fori_loop(0, n_tiles, zwait, 0)

    def row_copy(src_row, dst_row):
        return pltpu.make_async_copy(h_ref.at[pl.ds(src_row, 1)], xs_hbm.at[pl.ds(dst_row, 1)], sem)

    def start_body(t, carry):
        for k in range(TOP_K):
            row_copy(t, dest_ref[0, k, t]).start()
        return carry

    def wait_body(t, carry):
        for k in range(TOP_K):
            row_copy(0, 0).wait()
        return carry

    lax.fori_loop(0, tt, start_body, 0)
    lax.fori_loop(0, tt, wait_body, 0)


def _moe_gather(zflag, h2, dest, n_rows, *, tt, tm):
    t, d = h2.shape
    dest3 = dest.reshape(TOP_K, t // tt, tt).transpose(1, 0, 2)
    kern = functools.partial(_gather_kernel, tt=tt, tm=tm)
    return pl.pallas_call(
        kern,
        grid_spec=pltpu.PrefetchScalarGridSpec(
            num_scalar_prefetch=1,
            grid=(t // tt,),
            in_specs=[pl.BlockSpec((1, TOP_K, tt), lambda i, zf: (i, 0, 0), memory_space=pltpu.SMEM),
                      pl.BlockSpec((tt, d), lambda i, zf: (i, 0))],
            out_specs=pl.BlockSpec(memory_space=pl.ANY),
            scratch_shapes=[pltpu.VMEM((tm, d), h2.dtype), pltpu.SemaphoreType.DMA(()),
                            pltpu.SemaphoreType.DMA(())],
        ),
        out_shape=jax.ShapeDtypeStruct((n_rows, d), h2.dtype),
        compiler_params=_cparams("arbitrary"),
        name="moe_gather",
    )(zflag, dest3, h2)


def _expert_row_tiles(ts_ref, nt_ref, nv_ref, cnt_ref, src_hbm, dst_hbm, in_buf, out_buf, zero_ref, in_sem, out_sem, zsem,
                      compute):
    j, e = pl.program_id(0), pl.program_id(1)
    nj, ne = pl.num_programs(0), pl.num_programs(1)
    tm, tn = out_buf.shape[1], out_buf.shape[2]
    n_tiles = dst_hbm.shape[0] // tm
    nv = nv_ref[0]
    t0 = ts_ref[e]
    col = pl.multiple_of(j * tn, tn)

    def rows(g):
        return pl.ds(pl.multiple_of(g * tm, tm), tm)

    def in_copy(g, slot):
        return pltpu.make_async_copy(src_hbm.at[rows(g)], in_buf.at[slot], in_sem.at[slot])

    def out_copy(g, slot):
        return pltpu.make_async_copy(out_buf.at[slot], dst_hbm.at[rows(g), pl.ds(col, tn)], out_sem.at[slot])

    def zero_copy(g):
        return pltpu.make_async_copy(zero_ref, dst_hbm.at[rows(g), pl.ds(col, tn)], zsem)

    @pl.when(jnp.logical_and(j == 0, e == 0))
    def _():
        zero_ref[...] = jnp.zeros_like(zero_ref)

    def tile_body(t, carry):
        g = t0 + t
        seq = j * nv + g
        slot = seq % 2

        @pl.when(seq == 0)
        def _():
            in_copy(g, slot).start()

        in_copy(g, slot).wait()
        last_of_sweep = g + 1 == nv

        @pl.when(jnp.logical_or(jnp.logical_not(last_of_sweep), j + 1 < nj))
        def _():
            in_copy(jnp.where(last_of_sweep, 0, g + 1), 1 - slot).start()

        @pl.when(seq >= 2)
        def _():
            out_copy(g, slot).wait()

        half_rows = tm // 2
        few_rows = cnt_ref[e] - t * tm <= half_rows

        @pl.when(few_rows)
        def _():
            out_buf[slot, :half_rows] = compute(in_buf[slot, :half_rows])
            out_buf[slot, half_rows:] = jnp.zeros((tm - half_rows, tn), out_buf.dtype)

        @pl.when(jnp.logical_not(few_rows))
        def _():
            out_buf[slot] = compute(in_buf[slot])

        out_copy(g, slot).start()
        return carry

    lax.fori_loop(0, nt_ref[e], tile_body, 0)

    @pl.when(e == ne - 1)
    def _():
        def zstart(g, carry):
            zero_copy(g).start()
            return carry

        def zwait(g, carry):
            zero_copy(g).wait()
            return carry

        lax.fori_loop(nv, n_tiles, zstart, 0)
        lax.fori_loop(nv, n_tiles, zwait, 0)

    @pl.when(jnp.logical_and(e == ne - 1, j == nj - 1))
    def _():
        total = nj * nv

        @pl.when(total >= 1)
        def _():
            out_copy(0, (total - 1) % 2).wait()

        @pl.when(total >= 2)
        def _():
            out_copy(0, total % 2).wait()


def _expert_up_kernel(ts_ref, nt_ref, nv_ref, cnt_ref, wg_ref, bg_ref, wu_ref, bu_ref, xs_hbm, act_hbm,
                      wgc_ref, wuc_ref, in_buf, out_buf, zero_ref, in_sem, out_sem, zsem):
    @pl.when(nt_ref[pl.program_id(1)] > 0)
    def _():
        wgc_ref[...] = wg_ref[0].astype(BF16)
        wuc_ref[...] = wu_ref[0].astype(BF16)

    def compute(xw):
        half = xw.shape[1]
        x_lo = lax.bitcast_convert_type(xw << 16, F32).astype(BF16)
        x_hi = lax.bitcast_convert_type((xw >> 16) << 16, F32).astype(BF16)

        def proj(w_ref, b_ref):
            y = jnp.dot(x_lo, w_ref[:half, :], preferred_element_type=F32)
            return y + jnp.dot(x_hi, w_ref[half:, :], preferred_element_type=F32) + b_ref[0]

        gate = proj(wgc_ref, bg_ref)
        lin = proj(wuc_ref, bu_ref)
        gate = jnp.minimum(gate, SWIGLU_LIMIT)
        lin = jnp.clip(lin, -SWIGLU_LIMIT, SWIGLU_LIMIT)
        return (gate * jax.nn.sigmoid(SWIGLU_ALPHA * gate) * (lin + 1.0)).astype(BF16)

    _expert_row_tiles(ts_ref, nt_ref, nv_ref, cnt_ref, xs_hbm, act_hbm, in_buf, out_buf, zero_ref, in_sem, out_sem, zsem,
                      compute)


def _w_blk(j, e, ts, nt, nv, cnt):
    return (e, 0, j)


def _row_tile_scratch(tm, k_in, in_dtype, tn, out_dtype):
    return [pltpu.VMEM((2, tm, k_in), in_dtype), pltpu.VMEM((2, tm, tn), out_dtype), pltpu.VMEM((tm, tn), out_dtype),
            pltpu.SemaphoreType.DMA((2,)), pltpu.SemaphoreType.DMA((2,)), pltpu.SemaphoreType.DMA(())]


def _expert_up(plan, xs, wg, bg, wu, bu, *, tm, tn):
    r, dw = xs.shape
    n_e, d, de = wg.shape
    assert 2 * dw == d, "xs rows hold two bf16 columns per 32-bit word"
    return pl.pallas_call(
        _expert_up_kernel,
        grid_spec=pltpu.PrefetchScalarGridSpec(
            num_scalar_prefetch=4,
            grid=(de // tn, n_e),
            in_specs=[pl.BlockSpec((1, d, tn), _w_blk),
                      pl.BlockSpec((1, 1, tn), _w_blk),
                      pl.BlockSpec((1, d, tn), _w_blk),
                      pl.BlockSpec((1, 1, tn), _w_blk),
                      pl.BlockSpec(memory_space=pl.ANY)],
            out_specs=pl.BlockSpec(memory_space=pl.ANY),
            scratch_shapes=[pltpu.VMEM((d, tn), BF16), pltpu.VMEM((d, tn), BF16)]
            + _row_tile_scratch(tm, dw, xs.dtype, tn, BF16),
        ),
        out_shape=jax.ShapeDtypeStruct((r, de), BF16),
        compiler_params=_cparams("arbitrary", "arbitrary"),
        name="expert_up",
    )(*plan, wg, bg, wu, bu, xs)


def _expert_down_kernel(ts_ref, nt_ref, nv_ref, cnt_ref, w_ref, b_ref, act_hbm, ys_hbm,
                        wc_ref, in_buf, out_buf, zero_ref, in_sem, out_sem, zsem):
    @pl.when(nt_ref[pl.program_id(1)] > 0)
    def _():
        wc_ref[...] = w_ref[0].astype(BF16)

    def compute(a):
        return jnp.dot(a, wc_ref[...], preferred_element_type=F32) + b_ref[0]

    _expert_row_tiles(ts_ref, nt_ref, nv_ref, cnt_ref, act_hbm, ys_hbm, in_buf, out_buf, zero_ref, in_sem, out_sem, zsem,
                      compute)


def _expert_down(plan, act, wd, bd, *, tm, tn):
    r, de = act.shape
    n_e, _, d = wd.shape
    return pl.pallas_call(
        _expert_down_kernel,
        grid_spec=pltpu.PrefetchScalarGridSpec(
            num_scalar_prefetch=4,
            grid=(d // tn, n_e),
            in_specs=[pl.BlockSpec((1, de, tn), _w_blk),
                      pl.BlockSpec((1, 1, tn), _w_blk),
                      pl.BlockSpec(memory_space=pl.ANY)],
            out_specs=pl.BlockSpec(memory_space=pl.ANY),
            scratch_shapes=[pltpu.VMEM((de, tn), BF16)] + _row_tile_scratch(tm, de, act.dtype, tn, F32),
        ),
        out_shape=jax.ShapeDtypeStruct((r, d), F32),
        compiler_params=_cparams("arbitrary", "arbitrary"),
        name="expert_down",
    )(*plan, wd, bd, act)


def _combine_kernel(dest_ref, dest_next_ref, x_ref, g_ref, ys_hbm, o_ref, buf_ref, sem, *, tt):
    i = pl.program_id(0)
    slot = i % 2

    def row_copy(src_row, s, k, t):
        return pltpu.make_async_copy(ys_hbm.at[pl.ds(src_row, 1)], buf_ref.at[s, k, pl.ds(t, 1)], sem.at[s])

    def fetch(d_ref, s):
        def start_body(t, carry):
            for k in range(TOP_K):
                row_copy(d_ref[0, k, t], s, k, t).start()
            return carry

        lax.fori_loop(0, tt, start_body, 0)

    @pl.when(i == 0)
    def _():
        fetch(dest_ref, slot)

    @pl.when(i + 1 < pl.num_programs(0))
    def _():
        fetch(dest_next_ref, 1 - slot)

    def wait_body(t, carry):
        for k in range(TOP_K):
            row_copy(0, slot, k, t).wait()
        return carry

    lax.fori_loop(0, tt, wait_body, 0)
    g = g_ref[...]
    acc = x_ref[...]
    for k in range(TOP_K):
        acc = acc + g[:, k:k + 1] * buf_ref[slot, k]
    o_ref[...] = acc


def _moe_combine(dest, x1, gates_t, ys, *, tt):
    t, d = x1.shape
    n_steps = t // tt
    dest3 = dest.reshape(TOP_K, n_steps, tt).transpose(1, 0, 2)
    kern = functools.partial(_combine_kernel, tt=tt)
    return pl.pallas_call(
        kern,
        grid=(n_steps,),
        in_specs=[pl.BlockSpec((1, TOP_K, tt), lambda i: (i, 0, 0), memory_space=pltpu.SMEM),
                  pl.BlockSpec((1, TOP_K, tt), lambda i: (jnp.minimum(i + 1, n_steps - 1), 0, 0),
                               memory_space=pltpu.SMEM),
                  pl.BlockSpec((tt, d), lambda i: (i, 0)),
                  pl.BlockSpec((tt, TOP_K), lambda i: (i, 0)),
                  pl.BlockSpec(memory_space=pl.ANY)],
        out_specs=pl.BlockSpec((tt, d), lambda i: (i, 0)),
        out_shape=jax.ShapeDtypeStruct((t, d), F32),
        scratch_shapes=[pltpu.VMEM((2, TOP_K, tt, d), F32), pltpu.SemaphoreType.DMA((2,))],
        compiler_params=_cparams("arbitrary"),
        name="moe_combine",
    )(dest3, dest3, x1, gates_t, ys)


def _rope_tables(positions):
    half = QK_ROPE_DIM // 2
    inv_freq = ROPE_THETA ** (-jnp.arange(0, QK_ROPE_DIM, 2, dtype=F32) / QK_ROPE_DIM)
    ang = positions.astype(F32)[:, None] * inv_freq
    cos, sin = jnp.cos(ang), jnp.sin(ang)
    zeros = jnp.zeros((positions.shape[0], LANES - 2 * half), F32)
    return jnp.concatenate([cos, cos, zeros], axis=1), jnp.concatenate([-sin, sin, zeros], axis=1)


def _swap_halves(w):
    half = w.shape[-1] // 2
    return jnp.concatenate([w[..., half:], w[..., :half]], axis=-1)


def _layer(x, positions, attn_norm_w, w_in, q_a_norm_w, w_uq, kv_a_norm_w, w_ukv, q_head_norm_w, k_head_norm_w,
           conv_dw_w, conv_dw_b, conv_ln_w, conv_ln_b, attn_out_norm_w, conv_out_norm_w, w_o, ffn_norm_w,
           w_router, b_router, w_gate, b_gate, w_up, b_up, w_down, b_down):
    s, d = x.shape
    q_rank = q_a_norm_w.shape[0]
    kv_rank = kv_a_norm_w.shape[0]
    n_conv = conv_dw_b.shape[0]
    glu_w = 2 * n_conv
    de = w_gate.shape[2]

    pad64 = jnp.zeros((d, LANES - QK_ROPE_DIM), F32)
    w_kpe = w_in[:, glu_w + q_rank + kv_rank:]
    z_cols = glu_w + q_rank + kv_rank + 2 * LANES
    in_tn = 1024
    z_pad = -z_cols % in_tn
    w_in_p = jnp.concatenate(
        [w_in[:, :glu_w + q_rank + kv_rank], w_kpe, pad64, _swap_halves(w_kpe), pad64, jnp.zeros((d, z_pad), F32)],
        axis=1).astype(BF16)

    wq = w_uq.reshape(q_rank, N_HEADS, QK_HEAD_DIM)
    wq_pe = wq[:, :, QK_NOPE_DIM:]
    zq = jnp.zeros((q_rank, N_HEADS, LANES - QK_ROPE_DIM), F32)
    wq3 = jnp.concatenate([wq[:, :, :QK_NOPE_DIM], wq_pe, zq, _swap_halves(wq_pe), zq], axis=2)
    wq3 = wq3.transpose(1, 0, 2).astype(BF16)
    wkv3 = w_ukv.reshape(kv_rank, N_HEADS, QK_NOPE_DIM + V_HEAD_DIM).transpose(1, 0, 2).astype(BF16)
    zero_hw = jnp.zeros((QK_PAD_DIM - QK_HEAD_DIM,), F32)
    q_hw = jnp.concatenate([q_head_norm_w, zero_hw])[None]
    k_hw = jnp.concatenate([k_head_norm_w, zero_hw])[None]
    cos_t, sin_t = _rope_tables(positions)

    z = _norm_matmul(x, attn_norm_w[None], w_in_p, tm=512, tn=in_tn)
    conv_n = _conv_group(z, conv_dw_w, conv_dw_b, conv_ln_w, conv_ln_b, conv_out_norm_w, n_ch=n_conv, ts=256)
    q_scale = QK_HEAD_DIM ** -0.5 * LOG2E
    q = _q_proj(z, q_a_norm_w[None], wq3, cos_t, sin_t, q_hw, col_blk=glu_w // q_rank, tm=512, hb=4,
                q_scale=q_scale)
    k, v = _kv_proj(z, kv_a_norm_w[None], wkv3, cos_t, sin_t, k_hw, ckv_blk=(glu_w + q_rank) // kv_rank,
                    kpe_blk=(glu_w + q_rank + kv_rank) // LANES, tm=512, hb=4)
    attn = _attention(q, k, v, tq=512, tk=512)
    x1 = _out_proj(attn, conv_n, x, attn_out_norm_w[None], w_o.astype(BF16), tm=512, tn=1024)

    tm_e = 256
    h2, top_idx, gates, rank, counts = _router(x1, ffn_norm_w[None], w_router.T, b_router[:, None], tm=512)
    counts = counts[:, 0]
    tiles_per_e = (counts + tm_e - 1) // tm_e
    tile_end = jnp.cumsum(tiles_per_e)
    tile_start = tile_end - tiles_per_e
    n_valid = tile_end[-1:]
    expert_ids = jnp.arange(N_EXPERTS, dtype=I32)[:, None, None]
    row_start = (tile_start * tm_e)[:, None, None]
    dest = jnp.sum(jnp.where(top_idx[None] == expert_ids, row_start, 0), axis=0) + rank
    n_rows = s * TOP_K + N_EXPERTS * tm_e
    tile_ids = jnp.arange(n_rows // tm_e, dtype=I32)
    has_pad_rows = jnp.any((tile_ids[:, None] == tile_end[None, :] - 1) & (counts % tm_e != 0)[None, :], axis=1)
    zflag = (has_pad_rows | (tile_ids >= n_valid[0])).astype(I32)

    xs = _moe_gather(zflag, h2, dest, n_rows, tt=512, tm=tm_e)
    plan = (tile_start, tiles_per_e, n_valid, counts)
    act = _expert_up(plan, xs, w_gate, b_gate[:, None, :], w_up, b_up[:, None, :], tm=tm_e, tn=512)
    ys = _expert_down(plan, act, w_down, b_down[:, None, :], tm=tm_e, tn=2048)
    return _moe_combine(dest, x1, gates.T, ys, tt=128)


def kernel(x, positions, attn_norm_w, w_in, q_a_norm_w, w_uq, kv_a_norm_w, w_ukv, q_head_norm_w, k_head_norm_w,
           conv_dw_w, conv_dw_b, conv_ln_w, conv_ln_b, attn_out_norm_w, conv_out_norm_w, w_o, ffn_norm_w,
           w_router, b_router, w_gate, b_gate, w_up, b_up, w_down, b_down):
    b, s, d = x.shape
    depth = attn_norm_w.shape[0]
    assert b == 1, "one sequence per call"
    xs = x.reshape(s, d)
    for l in range(depth):
        xs = _layer(xs, positions[0], attn_norm_w[l], w_in[l], q_a_norm_w[l], w_uq[l], kv_a_norm_w[l], w_ukv[l],
                    q_head_norm_w[l], k_head_norm_w[l], conv_dw_w[l][:, 0, :], conv_dw_b[l], conv_ln_w[l],
                    conv_ln_b[l], attn_out_norm_w[l], conv_out_norm_w[l], w_o[l], ffn_norm_w[l], w_router[l],
                    b_router[l], w_gate[l], b_gate[l], w_up[l], b_up[l], w_down[l], b_down[l])
    return xs.reshape(b, s, d)
```

```python
import functools

import jax
import jax.numpy as jnp
from jax import lax
from jax.experimental import pallas as pl
from jax.experimental.pallas import tpu as pltpu

F32, BF16, I32, U32 = jnp.float32, jnp.bfloat16, jnp.int32, jnp.uint32

N_HEADS = 16
QK_NOPE_DIM = 128
QK_ROPE_DIM = 64
QK_HEAD_DIM = QK_NOPE_DIM + QK_ROPE_DIM
V_HEAD_DIM = 128
V_AUG_DIM = V_HEAD_DIM + 16
QK_PAD_DIM = 256
ROPE_THETA = 10000.0
CONV_WIDTH = 31
N_EXPERTS = 32
TOP_K = 4
SWIGLU_LIMIT = 7.0
SWIGLU_ALPHA = 1.702
EPS = 1e-6
LOG2E = 1.4426950408889634

LANES = 128
CONV_HALO = 16
VMEM_LIMIT_BYTES = 56 * 1024 * 1024


def _cparams(*sem):
    return pltpu.CompilerParams(dimension_semantics=sem, vmem_limit_bytes=VMEM_LIMIT_BYTES)


def _rms_scale(x, n):
    return lax.rsqrt(jnp.sum(x * x, axis=-1, keepdims=True) * (1.0 / n) + EPS)


def _norm_matmul_kernel(x_ref, nw_ref, w_ref, o_ref, h_ref):
    @pl.when(pl.program_id(1) == 0)
    def _():
        x = x_ref[...]
        h_ref[...] = (x * _rms_scale(x, x.shape[-1]) * nw_ref[...]).astype(BF16)

    o_ref[...] = jnp.dot(h_ref[...], w_ref[...], preferred_element_type=F32)


def _norm_matmul(x, nw, w, *, tm, tn):
    m, k = x.shape
    n = w.shape[1]
    return pl.pallas_call(
        _norm_matmul_kernel,
        grid=(m // tm, n // tn),
        in_specs=[pl.BlockSpec((tm, k), lambda i, j: (i, 0)),
                  pl.BlockSpec((1, k), lambda i, j: (0, 0)),
                  pl.BlockSpec((k, tn), lambda i, j: (0, j))],
        out_specs=pl.BlockSpec((tm, tn), lambda i, j: (i, j)),
        out_shape=jax.ShapeDtypeStruct((m, n), F32),
        scratch_shapes=[pltpu.VMEM((tm, k), BF16)],
        compiler_params=_cparams("parallel", "arbitrary"),
        name="in_proj",
    )(x, nw, w)


def _conv_kernel(a_ref, g_ref, ap_ref, gp_ref, an_ref, gn_ref, w_ref, b_ref, lnw_ref, lnb_ref, onw_ref,
                 o_ref, uext_ref, conv_ref, *, ts, nc, rb):
    i = pl.program_id(0)
    has_prev = i > 0
    has_next = i < pl.num_programs(0) - 1
    halo = CONV_HALO

    def glu(a, g):
        return a * jax.nn.sigmoid(g)

    for c in range(nc):
        sl = slice(c * LANES, (c + 1) * LANES)
        uext_ref[c, 0:halo, :] = jnp.where(has_prev, glu(ap_ref[:, sl], gp_ref[:, sl]), 0.0)
        uext_ref[c, halo:halo + ts, :] = glu(a_ref[:, sl], g_ref[:, sl])
        uext_ref[c, halo + ts:halo + ts + halo, :] = jnp.where(has_next, glu(an_ref[:, sl], gn_ref[:, sl]), 0.0)

    off0 = halo - CONV_WIDTH // 2

    def chunk_body(c, carry):
        w = w_ref[c]
        wk = [jnp.broadcast_to(w[k:k + 1, :], (rb, LANES)) for k in range(CONV_WIDTH)]
        bias = jnp.broadcast_to(b_ref[c], (rb, LANES))
        for r in range(ts // rb):
            acc = bias
            for k in range(CONV_WIDTH):
                acc = acc + wk[k] * uext_ref[c, r * rb + off0 + k:r * rb + off0 + k + rb, :]
            conv_ref[c, r * rb:(r + 1) * rb, :] = acc
        return carry

    lax.fori_loop(0, nc, chunk_body, 0)

    n_ch = nc * LANES
    xc = conv_ref[...]
    mu = jnp.sum(jnp.sum(xc, axis=0), axis=-1, keepdims=True) * (1.0 / n_ch)
    d = xc - mu[None]
    var = jnp.sum(jnp.sum(d * d, axis=0), axis=-1, keepdims=True) * (1.0 / n_ch)
    y = d * lax.rsqrt(var + EPS)[None] * lnw_ref[...] + lnb_ref[...]
    y = y * jax.nn.sigmoid(y)
    ms = jnp.sum(jnp.sum(y * y, axis=0), axis=-1, keepdims=True) * (1.0 / n_ch)
    o = y * lax.rsqrt(ms + EPS)[None] * onw_ref[...]
    for c in range(nc):
        o_ref[:, c * LANES:(c + 1) * LANES] = o[c].astype(BF16)


def _conv_group(z, dw_w, dw_b, ln_w, ln_b, on_w, *, n_ch, ts):
    s = z.shape[0]
    nc = n_ch // LANES
    hb = ts // CONV_HALO
    n_hblk = s // CONV_HALO

    def chunked(v):
        return v.reshape(nc, 1, LANES)

    w = dw_w.reshape(CONV_WIDTH, nc, LANES).transpose(1, 0, 2)
    w = jnp.pad(w, ((0, 0), (0, 32 - CONV_WIDTH), (0, 0)))
    vec_spec = pl.BlockSpec((nc, 1, LANES), lambda i: (0, 0, 0))
    kern = functools.partial(_conv_kernel, ts=ts, nc=nc, rb=64)
    return pl.pallas_call(
        kern,
        grid=(s // ts,),
        in_specs=[pl.BlockSpec((ts, n_ch), lambda i: (i, 0)),
                  pl.BlockSpec((ts, n_ch), lambda i: (i, 1)),
                  pl.BlockSpec((CONV_HALO, n_ch), lambda i: (jnp.maximum(i * hb - 1, 0), 0)),
                  pl.BlockSpec((CONV_HALO, n_ch), lambda i: (jnp.maximum(i * hb - 1, 0), 1)),
                  pl.BlockSpec((CONV_HALO, n_ch), lambda i: (jnp.minimum((i + 1) * hb, n_hblk - 1), 0)),
                  pl.BlockSpec((CONV_HALO, n_ch), lambda i: (jnp.minimum((i + 1) * hb, n_hblk - 1), 1)),
                  pl.BlockSpec((nc, 32, LANES), lambda i: (0, 0, 0)),
                  vec_spec, vec_spec, vec_spec, vec_spec],
        out_specs=pl.BlockSpec((ts, n_ch), lambda i: (i, 0)),
        out_shape=jax.ShapeDtypeStruct((s, n_ch), BF16),
        scratch_shapes=[pltpu.VMEM((nc, ts + 2 * CONV_HALO, LANES), F32),
                        pltpu.VMEM((nc, ts, LANES), F32)],
        compiler_params=_cparams("parallel"),
        name="conv_group",
    )(z, z, z, z, z, z, w, chunked(dw_b), chunked(ln_w), chunked(ln_b), chunked(on_w))


def _q_kernel(z_ref, nw_ref, w_ref, cos_ref, sin_ref, hw_ref, q_ref, zn_ref, *, q_scale):
    @pl.when(pl.program_id(1) == 0)
    def _():
        z = z_ref[...]
        zn_ref[...] = (z * _rms_scale(z, z.shape[-1]) * nw_ref[...]).astype(BF16)

    hw = hw_ref[...]
    for hh in range(w_ref.shape[0]):
        r = jnp.dot(zn_ref[...], w_ref[hh], preferred_element_type=F32)
        nope = r[:, :LANES]
        rope = r[:, LANES:2 * LANES] * cos_ref[...] + r[:, 2 * LANES:] * sin_ref[...]
        ss = jnp.sum(nope * nope, axis=-1, keepdims=True) + jnp.sum(rope * rope, axis=-1, keepdims=True)
        inv = lax.rsqrt(ss * (1.0 / QK_HEAD_DIM) + EPS) * q_scale
        q_ref[hh, :LANES, :] = (nope * inv * hw[:, :LANES]).T.astype(BF16)
        q_ref[hh, LANES:, :] = (rope * inv * hw[:, LANES:]).T.astype(BF16)


def _q_proj(z, nw, w3, cos_t, sin_t, hw, *, col_blk, tm, hb, q_scale):
    s = z.shape[0]
    rank = w3.shape[1]
    kern = functools.partial(_q_kernel, q_scale=q_scale)
    return pl.pallas_call(
        kern,
        grid=(s // tm, N_HEADS // hb),
        in_specs=[pl.BlockSpec((tm, rank), lambda i, h: (i, col_blk)),
                  pl.BlockSpec((1, rank), lambda i, h: (0, 0)),
                  pl.BlockSpec((hb, rank, 3 * LANES), lambda i, h: (h, 0, 0)),
                  pl.BlockSpec((tm, LANES), lambda i, h: (i, 0)),
                  pl.BlockSpec((tm, LANES), lambda i, h: (i, 0)),
                  pl.BlockSpec((1, QK_PAD_DIM), lambda i, h: (0, 0))],
        out_specs=pl.BlockSpec((hb, QK_PAD_DIM, tm), lambda i, h: (h, 0, i)),
        out_shape=jax.ShapeDtypeStruct((N_HEADS, QK_PAD_DIM, s), BF16),
        scratch_shapes=[pltpu.VMEM((tm, rank), BF16)],
        compiler_params=_cparams("parallel", "arbitrary"),
        name="q_proj",
    )(z, nw, w3, cos_t, sin_t, hw)


def _kv_kernel(z_ref, kpe_ref, kpes_ref, nw_ref, w_ref, cos_ref, sin_ref, hw_ref, k_ref, v_ref,
               zn_ref, kr_ref, ss_ref):
    @pl.when(pl.program_id(1) == 0)
    def _():
        z = z_ref[...]
        zn_ref[...] = (z * _rms_scale(z, z.shape[-1]) * nw_ref[...]).astype(BF16)
        kr = kpe_ref[...] * cos_ref[...] + kpes_ref[...] * sin_ref[...]
        kr_ref[...] = kr
        ss_ref[...] = jnp.sum(kr * kr, axis=-1, keepdims=True)

    hw = hw_ref[...]
    for hh in range(w_ref.shape[0]):
        r = jnp.dot(zn_ref[...], w_ref[hh], preferred_element_type=F32)
        kn = r[:, :LANES]
        ss = jnp.sum(kn * kn, axis=-1, keepdims=True) + ss_ref[...]
        inv = lax.rsqrt(ss * (1.0 / QK_HEAD_DIM) + EPS)
        k_ref[hh, :, :LANES] = (kn * inv * hw[:, :LANES]).astype(BF16)
        k_ref[hh, :, LANES:] = (kr_ref[...] * inv * hw[:, LANES:]).astype(BF16)
        v_ref[hh, :V_HEAD_DIM, :] = r[:, LANES:].T.astype(BF16)
        ones_row = lax.broadcasted_iota(I32, (V_AUG_DIM - V_HEAD_DIM, r.shape[0]), 0) == 0
        v_ref[hh, V_HEAD_DIM:, :] = jnp.where(ones_row, 1.0, 0.0).astype(BF16)


def _kv_proj(z, nw, w3, cos_t, sin_t, hw, *, ckv_blk, kpe_blk, tm, hb):
    s = z.shape[0]
    rank = w3.shape[1]
    return pl.pallas_call(
        _kv_kernel,
        grid=(s // tm, N_HEADS // hb),
        in_specs=[pl.BlockSpec((tm, rank), lambda i, h: (i, ckv_blk)),
                  pl.BlockSpec((tm, LANES), lambda i, h: (i, kpe_blk)),
                  pl.BlockSpec((tm, LANES), lambda i, h: (i, kpe_blk + 1)),
                  pl.BlockSpec((1, rank), lambda i, h: (0, 0)),
                  pl.BlockSpec((hb, rank, 2 * LANES), lambda i, h: (h, 0, 0)),
                  pl.BlockSpec((tm, LANES), lambda i, h: (i, 0)),
                  pl.BlockSpec((tm, LANES), lambda i, h: (i, 0)),
                  pl.BlockSpec((1, QK_PAD_DIM), lambda i, h: (0, 0))],
        out_specs=[pl.BlockSpec((hb, tm, QK_PAD_DIM), lambda i, h: (h, i, 0)),
                   pl.BlockSpec((hb, V_AUG_DIM, tm), lambda i, h: (h, 0, i))],
        out_shape=[jax.ShapeDtypeStruct((N_HEADS, s, QK_PAD_DIM), BF16),
                   jax.ShapeDtypeStruct((N_HEADS, V_AUG_DIM, s), BF16)],
        scratch_shapes=[pltpu.VMEM((tm, rank), BF16),
                        pltpu.VMEM((tm, LANES), F32),
                        pltpu.VMEM((tm, 1), F32)],
        compiler_params=_cparams("parallel", "arbitrary"),
        name="kv_proj",
    )(z, z, z, nw, w3, cos_t, sin_t, hw)


def _attn_kernel(qt_ref, k_ref, vt_ref, o_ref, st_ref, *, tk):
    qt = qt_ref[0]
    tq = qt.shape[1]
    nk = k_ref.shape[1] // tk

    def scores(j, slot):
        start = pl.multiple_of(j * tk, tk)
        st_ref[slot] = jnp.dot(k_ref[0, pl.ds(start, tk), :], qt, preferred_element_type=F32)

    def update(j, slot, carry):
        m, acc = carry
        start = pl.multiple_of(j * tk, tk)
        vts = vt_ref[0, :, pl.ds(start, tk)]
        st = st_ref[slot]
        m_new = jnp.maximum(m, jnp.max(st, axis=0, keepdims=True))
        p = jnp.exp2(st - m_new)
        alpha = jnp.exp2(m - m_new)
        acc = alpha * acc + jnp.dot(vts, p.astype(BF16), preferred_element_type=F32)
        return m_new, acc

    def body(jj, carry):
        j = 2 * jj
        scores(j + 1, 1)
        carry = update(j, 0, carry)
        scores(jnp.minimum(j + 2, nk - 1), 0)
        return update(j + 1, 1, carry)

    m0 = jnp.full((1, tq), -jnp.inf, F32)
    acc0 = jnp.zeros((V_AUG_DIM, tq), F32)
    scores(0, 0)
    _, acc = lax.fori_loop(0, nk // 2, body, (m0, acc0), unroll=2)
    o_ref[...] = (acc[:V_HEAD_DIM] / acc[V_HEAD_DIM:V_HEAD_DIM + 1]).T


def _attention(qt, k, vt, *, tq, tk):
    h, s, _ = k.shape
    kern = functools.partial(_attn_kernel, tk=tk)
    return pl.pallas_call(
        kern,
        grid=(h, s // tq),
        in_specs=[pl.BlockSpec((1, QK_PAD_DIM, tq), lambda hh, i: (hh, 0, i)),
                  pl.BlockSpec((1, s, QK_PAD_DIM), lambda hh, i: (hh, 0, 0)),
                  pl.BlockSpec((1, V_AUG_DIM, s), lambda hh, i: (hh, 0, 0))],
        out_specs=pl.BlockSpec((tq, V_HEAD_DIM), lambda hh, i: (i, hh)),
        out_shape=jax.ShapeDtypeStruct((s, h * V_HEAD_DIM), F32),
        scratch_shapes=[pltpu.VMEM((2, tk, tq), F32)],
        compiler_params=_cparams("parallel", "arbitrary"),
        name="attention",
    )(qt, k, vt)


def _out_proj_kernel(a_ref, c_ref, x_ref, nw_ref, wa_ref, wc_ref, o_ref, an_ref):
    @pl.when(pl.program_id(1) == 0)
    def _():
        a = a_ref[...]
        an_ref[...] = (a * _rms_scale(a, a.shape[-1]) * nw_ref[...]).astype(BF16)

    acc = jnp.dot(an_ref[...], wa_ref[...], preferred_element_type=F32)
    acc = acc + jnp.dot(c_ref[...], wc_ref[...], preferred_element_type=F32)
    o_ref[...] = x_ref[...] + acc


def _out_proj(attn, conv_n, x, nw, w_o, *, tm, tn):
    s, da = attn.shape
    dc = conv_n.shape[1]
    d = x.shape[1]
    assert da == dc, "the two head groups share one W_o row-block size"
    return pl.pallas_call(
        _out_proj_kernel,
        grid=(s // tm, d // tn),
        in_specs=[pl.BlockSpec((tm, da), lambda i, j: (i, 0)),
                  pl.BlockSpec((tm, dc), lambda i, j: (i, 0)),
                  pl.BlockSpec((tm, tn), lambda i, j: (i, j)),
                  pl.BlockSpec((1, da), lambda i, j: (0, 0)),
                  pl.BlockSpec((da, tn), lambda i, j: (0, j)),
                  pl.BlockSpec((dc, tn), lambda i, j: (1, j))],
        out_specs=pl.BlockSpec((tm, tn), lambda i, j: (i, j)),
        out_shape=jax.ShapeDtypeStruct((s, d), F32),
        scratch_shapes=[pltpu.VMEM((tm, da), BF16)],
        compiler_params=_cparams("parallel", "arbitrary"),
        name="out_proj",
    )(attn, conv_n, x, nw, w_o, w_o)


def _router_kernel(x_ref, nw_ref, wr_ref, br_ref, tri_ref, h_ref, idx_ref, gate_ref, rank_ref, cnt_ref, run_ref):
    @pl.when(pl.program_id(0) == 0)
    def _():
        run_ref[...] = jnp.zeros_like(run_ref)

    x = x_ref[...]
    h = x * _rms_scale(x, x.shape[-1]) * nw_ref[...]
    half = h.shape[1] // 2
    lo = lax.bitcast_convert_type(h[:, :half].astype(BF16).astype(F32), U32)
    hi = lax.bitcast_convert_type(h[:, half:].astype(BF16).astype(F32), U32)
    h_ref[...] = (lo >> 16) | hi
    tm = x.shape[0]
    logits = lax.dot_general(wr_ref[...], h, (((1,), (1,)), ((), ())), preferred_element_type=F32,
                             precision=lax.Precision.HIGHEST) + br_ref[...]
    eio = lax.broadcasted_iota(I32, (N_EXPERTS, tm), 0)
    vals, idxs = [], []
    cur = logits
    for _ in range(TOP_K):
        m = jnp.max(cur, axis=0, keepdims=True)
        idx = jnp.min(jnp.where(cur == m, eio, N_EXPERTS), axis=0, keepdims=True)
        vals.append(m)
        idxs.append(idx)
        cur = jnp.where(eio == idx, -jnp.inf, cur)
    es = [jnp.exp(v - vals[0]) for v in vals]
    den = es[0] + es[1] + es[2] + es[3]
    base = run_ref[...]
    for k in range(TOP_K):
        gate_ref[k:k + 1, :] = es[k] / den
        idx_ref[k:k + 1, :] = idxs[k]
        onehot = eio == idxs[k]
        before = jnp.dot(onehot.astype(BF16), tri_ref[...], preferred_element_type=F32)
        rank = jnp.sum(jnp.where(onehot, before + base, 0.0), axis=0, keepdims=True)
        rank_ref[k:k + 1, :] = rank.astype(I32)
        base = base + jnp.sum(onehot.astype(F32), axis=1, keepdims=True)
    run_ref[...] = base
    cnt_ref[...] = base.astype(I32)


def _router(x1, nw, wr_t, br, *, tm):
    t, d = x1.shape
    tri = (lax.broadcasted_iota(I32, (tm, tm), 0) < lax.broadcasted_iota(I32, (tm, tm), 1)).astype(BF16)
    row4 = pl.BlockSpec((TOP_K, tm), lambda i: (0, i))
    return pl.pallas_call(
        _router_kernel,
        grid=(t // tm,),
        in_specs=[pl.BlockSpec((tm, d), lambda i: (i, 0)),
                  pl.BlockSpec((1, d), lambda i: (0, 0)),
                  pl.BlockSpec((N_EXPERTS, d), lambda i: (0, 0)),
                  pl.BlockSpec((N_EXPERTS, 1), lambda i: (0, 0)),
                  pl.BlockSpec((tm, tm), lambda i: (0, 0))],
        out_specs=[pl.BlockSpec((tm, d // 2), lambda i: (i, 0)), row4, row4, row4,
                   pl.BlockSpec((N_EXPERTS, 1), lambda i: (0, 0))],
        out_shape=[jax.ShapeDtypeStruct((t, d // 2), U32),
                   jax.ShapeDtypeStruct((TOP_K, t), I32),
                   jax.ShapeDtypeStruct((TOP_K, t), F32),
                   jax.ShapeDtypeStruct((TOP_K, t), I32),
                   jax.ShapeDtypeStruct((N_EXPERTS, 1), I32)],
        scratch_shapes=[pltpu.VMEM((N_EXPERTS, 1), F32)],
        compiler_params=_cparams("arbitrary"),
        name="router",
    )(x1, nw, wr_t, br, tri)


def _gather_kernel(zflag_ref, dest_ref, h_ref, xs_hbm, zero_ref, sem, zsem, *, tt, tm):
    n_tiles = xs_hbm.shape[0] // tm

    @pl.when(pl.program_id(0) == 0)
    def _():
        zero_ref[...] = jnp.zeros_like(zero_ref)

        def zero_copy(g):
            return pltpu.make_async_copy(zero_ref, xs_hbm.at[pl.ds(pl.multiple_of(g * tm, tm), tm)], zsem)

        def zstart(g, carry):
            @pl.when(zflag_ref[g] != 0)
            def _():
                zero_copy(g).start()
            return carry

        def zwait(g, carry):
            @pl.when(zflag_ref[g] != 0)
            def _():
                zero_copy(g).wait()
            return carry

        lax.fori_loop(0, n_tiles, zstart, 0)
        lax.fori_loop(0, n_tiles, zwait, 0)

    def row_copy(src_row, dst_row):
        return pltpu.make_async_copy(h_ref.at[pl.ds(src_row, 1)], xs_hbm.at[pl.ds(dst_row, 1)], sem)

    def start_body(t, carry):
        for k in range(TOP_K):
            row_copy(t, dest_ref[0, k, t]).start()
        return carry

    def wait_body(t, carry):
        for k in range(TOP_K):
            row_copy(0, 0).wait()
        return carry

    lax.fori_loop(0, tt, start_body, 0)
    lax.fori_loop(0, tt, wait_body, 0)


def _moe_gather(zflag, h2, dest, n_rows, *, tt, tm):
    t, d = h2.shape
    dest3 = dest.reshape(TOP_K, t // tt, tt).transpose(1, 0, 2)
    kern = functools.partial(_gather_kernel, tt=tt, tm=tm)
    return pl.pallas_call(
        kern,
        grid_spec=pltpu.PrefetchScalarGridSpec(
            num_scalar_prefetch=1,
            grid=(t // tt,),
            in_specs=[pl.BlockSpec((1, TOP_K, tt), lambda i, zf: (i, 0, 0), memory_space=pltpu.SMEM),
                      pl.BlockSpec((tt, d), lambda i, zf: (i, 0))],
            out_specs=pl.BlockSpec(memory_space=pl.ANY),
            scratch_shapes=[pltpu.VMEM((tm, d), h2.dtype), pltpu.SemaphoreType.DMA(()),
                            pltpu.SemaphoreType.DMA(())],
        ),
        out_shape=jax.ShapeDtypeStruct((n_rows, d), h2.dtype),
        compiler_params=_cparams("arbitrary"),
        name="moe_gather",
    )(zflag, dest3, h2)


def _expert_row_tiles(ts_ref, nt_ref, nv_ref, cnt_ref, src_hbm, dst_hbm, in_buf, out_buf, zero_ref, in_sem, out_sem, zsem,
                      compute):
    j, e = pl.program_id(0), pl.program_id(1)
    nj, ne = pl.num_programs(0), pl.num_programs(1)
    tm, tn = out_buf.shape[1], out_buf.shape[2]
    n_tiles = dst_hbm.shape[0] // tm
    nv = nv_ref[0]
    t0 = ts_ref[e]
    col = pl.multiple_of(j * tn, tn)

    def rows(g):
        return pl.ds(pl.multiple_of(g * tm, tm), tm)

    def in_copy(g, slot):
        return pltpu.make_async_copy(src_hbm.at[rows(g)], in_buf.at[slot], in_sem.at[slot])

    def out_copy(g, slot):
        return pltpu.make_async_copy(out_buf.at[slot], dst_hbm.at[rows(g), pl.ds(col, tn)], out_sem.at[slot])

    def zero_copy(g):
        return pltpu.make_async_copy(zero_ref, dst_hbm.at[rows(g), pl.ds(col, tn)], zsem)

    @pl.when(jnp.logical_and(j == 0, e == 0))
    def _():
        zero_ref[...] = jnp.zeros_like(zero_ref)

    def tile_body(t, carry):
        g = t0 + t
        seq = j * nv + g
        slot = seq % 2

        @pl.when(seq == 0)
        def _():
            in_copy(g, slot).start()

        in_copy(g, slot).wait()
        last_of_sweep = g + 1 == nv

        @pl.when(jnp.logical_or(jnp.logical_not(last_of_sweep), j + 1 < nj))
        def _():
            in_copy(jnp.where(last_of_sweep, 0, g + 1), 1 - slot).start()

        @pl.when(seq >= 2)
        def _():
            out_copy(g, slot).wait()

        half_rows = tm // 2
        few_rows = cnt_ref[e] - t * tm <= half_rows

        @pl.when(few_rows)
        def _():
            out_buf[slot, :half_rows] = compute(in_buf[slot, :half_rows])
            out_buf[slot, half_rows:] = jnp.zeros((tm - half_rows, tn), out_buf.dtype)

        @pl.when(jnp.logical_not(few_rows))
        def _():
            out_buf[slot] = compute(in_buf[slot])

        out_copy(g, slot).start()
        return carry

    lax.fori_loop(0, nt_ref[e], tile_body, 0)

    @pl.when(e == ne - 1)
    def _():
        def zstart(g, carry):
            zero_copy(g).start()
            return carry

        def zwait(g, carry):
            zero_copy(g).wait()
            return carry

        lax.fori_loop(nv, n_tiles, zstart, 0)
        lax.fori_loop(nv, n_tiles, zwait, 0)

    @pl.when(jnp.logical_and(e == ne - 1, j == nj - 1))
    def _():
        total = nj * nv

        @pl.when(total >= 1)
        def _():
            out_copy(0, (total - 1) % 2).wait()

        @pl.when(total >= 2)
        def _():
            out_copy(0, total % 2).wait()


def _expert_up_kernel(ts_ref, nt_ref, nv_ref, cnt_ref, wg_ref, bg_ref, wu_ref, bu_ref, xs_hbm, act_hbm,
                      wgc_ref, wuc_ref, in_buf, out_buf, zero_ref, in_sem, out_sem, zsem):
    @pl.when(nt_ref[pl.program_id(1)] > 0)
    def _():
        wgc_ref[...] = wg_ref[0].astype(BF16)
        wuc_ref[...] = wu_ref[0].astype(BF16)

    def compute(xw):
        half = xw.shape[1]
        x_lo = lax.bitcast_convert_type(xw << 16, F32).astype(BF16)
        x_hi = lax.bitcast_convert_type((xw >> 16) << 16, F32).astype(BF16)

        def proj(w_ref, b_ref):
            y = jnp.dot(x_lo, w_ref[:half, :], preferred_element_type=F32)
            return y + jnp.dot(x_hi, w_ref[half:, :], preferred_element_type=F32) + b_ref[0]

        gate = proj(wgc_ref, bg_ref)
        lin = proj(wuc_ref, bu_ref)
        gate = jnp.minimum(gate, SWIGLU_LIMIT)
        lin = jnp.clip(lin, -SWIGLU_LIMIT, SWIGLU_LIMIT)
        return (gate * jax.nn.sigmoid(SWIGLU_ALPHA * gate) * (lin + 1.0)).astype(BF16)

    _expert_row_tiles(ts_ref, nt_ref, nv_ref, cnt_ref, xs_hbm, act_hbm, in_buf, out_buf, zero_ref, in_sem, out_sem, zsem,
                      compute)


def _w_blk(j, e, ts, nt, nv, cnt):
    return (e, 0, j)


def _row_tile_scratch(tm, k_in, in_dtype, tn, out_dtype):
    return [pltpu.VMEM((2, tm, k_in), in_dtype), pltpu.VMEM((2, tm, tn), out_dtype), pltpu.VMEM((tm, tn), out_dtype),
            pltpu.SemaphoreType.DMA((2,)), pltpu.SemaphoreType.DMA((2,)), pltpu.SemaphoreType.DMA(())]


def _expert_up(plan, xs, wg, bg, wu, bu, *, tm, tn):
    r, dw = xs.shape
    n_e, d, de = wg.shape
    assert 2 * dw == d, "xs rows hold two bf16 columns per 32-bit word"
    return pl.pallas_call(
        _expert_up_kernel,
        grid_spec=pltpu.PrefetchScalarGridSpec(
            num_scalar_prefetch=4,
            grid=(de // tn, n_e),
            in_specs=[pl.BlockSpec((1, d, tn), _w_blk),
                      pl.BlockSpec((1, 1, tn), _w_blk),
                      pl.BlockSpec((1, d, tn), _w_blk),
                      pl.BlockSpec((1, 1, tn), _w_blk),
                      pl.BlockSpec(memory_space=pl.ANY)],
            out_specs=pl.BlockSpec(memory_space=pl.ANY),
            scratch_shapes=[pltpu.VMEM((d, tn), BF16), pltpu.VMEM((d, tn), BF16)]
            + _row_tile_scratch(tm, dw, xs.dtype, tn, BF16),
        ),
        out_shape=jax.ShapeDtypeStruct((r, de), BF16),
        compiler_params=_cparams("arbitrary", "arbitrary"),
        name="expert_up",
    )(*plan, wg, bg, wu, bu, xs)


def _expert_down_kernel(ts_ref, nt_ref, nv_ref, cnt_ref, w_ref, b_ref, act_hbm, ys_hbm,
                        wc_ref, in_buf, out_buf, zero_ref, in_sem, out_sem, zsem):
    @pl.when(nt_ref[pl.program_id(1)] > 0)
    def _():
        wc_ref[...] = w_ref[0].astype(BF16)

    def compute(a):
        return jnp.dot(a, wc_ref[...], preferred_element_type=F32) + b_ref[0]

    _expert_row_tiles(ts_ref, nt_ref, nv_ref, cnt_ref, act_hbm, ys_hbm, in_buf, out_buf, zero_ref, in_sem, out_sem, zsem,
                      compute)


def _expert_down(plan, act, wd, bd, *, tm, tn):
    r, de = act.shape
    n_e, _, d = wd.shape
    return pl.pallas_call(
        _expert_down_kernel,
        grid_spec=pltpu.PrefetchScalarGridSpec(
            num_scalar_prefetch=4,
            grid=(d // tn, n_e),
            in_specs=[pl.BlockSpec((1, de, tn), _w_blk),
                      pl.BlockSpec((1, 1, tn), _w_blk),
                      pl.BlockSpec(memory_space=pl.ANY)],
            out_specs=pl.BlockSpec(memory_space=pl.ANY),
            scratch_shapes=[pltpu.VMEM((de, tn), BF16)] + _row_tile_scratch(tm, de, act.dtype, tn, F32),
        ),
        out_shape=jax.ShapeDtypeStruct((r, d), F32),
        compiler_params=_cparams("arbitrary", "arbitrary"),
        name="expert_down",
    )(*plan, wd, bd, act)


def _combine_kernel(dest_ref, dest_next_ref, x_ref, g_ref, ys_hbm, o_ref, buf_ref, sem, *, tt):
    i = pl.program_id(0)
    slot = i % 2

    def row_copy(src_row, s, k, t):
        return pltpu.make_async_copy(ys_hbm.at[pl.ds(src_row, 1)], buf_ref.at[s, k, pl.ds(t, 1)], sem.at[s])

    def fetch(d_ref, s):
        def start_body(t, carry):
            for k in range(TOP_K):
                row_copy(d_ref[0, k, t], s, k, t).start()
            return carry

        lax.fori_loop(0, tt, start_body, 0)

    @pl.when(i == 0)
    def _():
        fetch(dest_ref, slot)

    @pl.when(i + 1 < pl.num_programs(0))
    def _():
        fetch(dest_next_ref, 1 - slot)

    def wait_body(t, carry):
        for k in range(TOP_K):
            row_copy(0, slot, k, t).wait()
        return carry

    lax.fori_loop(0, tt, wait_body, 0)
    g = g_ref[...]
    acc = x_ref[...]
    for k in range(TOP_K):
        acc = acc + g[:, k:k + 1] * buf_ref[slot, k]
    o_ref[...] = acc


def _moe_combine(dest, x1, gates_t, ys, *, tt):
    t, d = x1.shape
    n_steps = t // tt
    dest3 = dest.reshape(TOP_K, n_steps, tt).transpose(1, 0, 2)
    kern = functools.partial(_combine_kernel, tt=tt)
    return pl.pallas_call(
        kern,
        grid=(n_steps,),
        in_specs=[pl.BlockSpec((1, TOP_K, tt), lambda i: (i, 0, 0), memory_space=pltpu.SMEM),
                  pl.BlockSpec((1, TOP_K, tt), lambda i: (jnp.minimum(i + 1, n_steps - 1), 0, 0),
                               memory_space=pltpu.SMEM),
                  pl.BlockSpec((tt, d), lambda i: (i, 0)),
                  pl.BlockSpec((tt, TOP_K), lambda i: (i, 0)),
                  pl.BlockSpec(memory_space=pl.ANY)],
        out_specs=pl.BlockSpec((tt, d), lambda i: (i, 0)),
        out_shape=jax.ShapeDtypeStruct((t, d), F32),
        scratch_shapes=[pltpu.VMEM((2, TOP_K, tt, d), F32), pltpu.SemaphoreType.DMA((2,))],
        compiler_params=_cparams("arbitrary"),
        name="moe_combine",
    )(dest3, dest3, x1, gates_t, ys)


def _rope_tables(positions):
    half = QK_ROPE_DIM // 2
    inv_freq = ROPE_THETA ** (-jnp.arange(0, QK_ROPE_DIM, 2, dtype=F32) / QK_ROPE_DIM)
    ang = positions.astype(F32)[:, None] * inv_freq
    cos, sin = jnp.cos(ang), jnp.sin(ang)
    zeros = jnp.zeros((positions.shape[0], LANES - 2 * half), F32)
    return jnp.concatenate([cos, cos, zeros], axis=1), jnp.concatenate([-sin, sin, zeros], axis=1)


def _swap_halves(w):
    half = w.shape[-1] // 2
    return jnp.concatenate([w[..., half:], w[..., :half]], axis=-1)


def _layer(x, positions, attn_norm_w, w_in, q_a_norm_w, w_uq, kv_a_norm_w, w_ukv, q_head_norm_w, k_head_norm_w,
           conv_dw_w, conv_dw_b, conv_ln_w, conv_ln_b, attn_out_norm_w, conv_out_norm_w, w_o, ffn_norm_w,
           w_router, b_router, w_gate, b_gate, w_up, b_up, w_down, b_down):
    s, d = x.shape
    q_rank = q_a_norm_w.shape[0]
    kv_rank = kv_a_norm_w.shape[0]
    n_conv = conv_dw_b.shape[0]
    glu_w = 2 * n_conv
    de = w_gate.shape[2]

    pad64 = jnp.zeros((d, LANES - QK_ROPE_DIM), F32)
    w_kpe = w_in[:, glu_w + q_rank + kv_rank:]
    z_cols = glu_w + q_rank + kv_rank + 2 * LANES
    in_tn = 1024
    z_pad = -z_cols % in_tn
    w_in_p = jnp.concatenate(
        [w_in[:, :glu_w + q_rank + kv_rank], w_kpe, pad64, _swap_halves(w_kpe), pad64, jnp.zeros((d, z_pad), F32)],
        axis=1).astype(BF16)

    wq = w_uq.reshape(q_rank, N_HEADS, QK_HEAD_DIM)
    wq_pe = wq[:, :, QK_NOPE_DIM:]
    zq = jnp.zeros((q_rank, N_HEADS, LANES - QK_ROPE_DIM), F32)
    wq3 = jnp.concatenate([wq[:, :, :QK_NOPE_DIM], wq_pe, zq, _swap_halves(wq_pe), zq], axis=2)
    wq3 = wq3.transpose(1, 0, 2).astype(BF16)
    wkv3 = w_ukv.reshape(kv_rank, N_HEADS, QK_NOPE_DIM + V_HEAD_DIM).transpose(1, 0, 2).astype(BF16)
    zero_hw = jnp.zeros((QK_PAD_DIM - QK_HEAD_DIM,), F32)
    q_hw = jnp.concatenate([q_head_norm_w, zero_hw])[None]
    k_hw = jnp.concatenate([k_head_norm_w, zero_hw])[None]
    cos_t, sin_t = _rope_tables(positions)

    z = _norm_matmul(x, attn_norm_w[None], w_in_p, tm=512, tn=in_tn)
    conv_n = _conv_group(z, conv_dw_w, conv_dw_b, conv_ln_w, conv_ln_b, conv_out_norm_w, n_ch=n_conv, ts=256)
    q_scale = QK_HEAD_DIM ** -0.5 * LOG2E
    q = _q_proj(z, q_a_norm_w[None], wq3, cos_t, sin_t, q_hw, col_blk=glu_w // q_rank, tm=512, hb=8,
                q_scale=q_scale)
    k, v = _kv_proj(z, kv_a_norm_w[None], wkv3, cos_t, sin_t, k_hw, ckv_blk=(glu_w + q_rank) // kv_rank,
                    kpe_blk=(glu_w + q_rank + kv_rank) // LANES, tm=512, hb=8)
    attn = _attention(q, k, v, tq=512, tk=512)
    x1 = _out_proj(attn, conv_n, x, attn_out_norm_w[None], w_o.astype(BF16), tm=512, tn=1024)

    tm_e = 256
    h2, top_idx, gates, rank, counts = _router(x1, ffn_norm_w[None], w_router.T, b_router[:, None], tm=512)
    counts = counts[:, 0]
    tiles_per_e = (counts + tm_e - 1) // tm_e
    tile_end = jnp.cumsum(tiles_per_e)
    tile_start = tile_end - tiles_per_e
    n_valid = tile_end[-1:]
    expert_ids = jnp.arange(N_EXPERTS, dtype=I32)[:, None, None]
    row_start = (tile_start * tm_e)[:, None, None]
    dest = jnp.sum(jnp.where(top_idx[None] == expert_ids, row_start, 0), axis=0) + rank
    n_rows = s * TOP_K + N_EXPERTS * tm_e
    tile_ids = jnp.arange(n_rows // tm_e, dtype=I32)
    has_pad_rows = jnp.any((tile_ids[:, None] == tile_end[None, :] - 1) & (counts % tm_e != 0)[None, :], axis=1)
    zflag = (has_pad_rows | (tile_ids >= n_valid[0])).astype(I32)

    xs = _moe_gather(zflag, h2, dest, n_rows, tt=512, tm=tm_e)
    plan = (tile_start, tiles_per_e, n_valid, counts)
    act = _expert_up(plan, xs, w_gate, b_gate[:, None, :], w_up, b_up[:, None, :], tm=tm_e, tn=512)
    ys = _expert_down(plan, act, w_down, b_down[:, None, :], tm=tm_e, tn=2048)
    return _moe_combine(dest, x1, gates.T, ys, tt=128)


def kernel(x, positions, attn_norm_w, w_in, q_a_norm_w, w_uq, kv_a_norm_w, w_ukv, q_head_norm_w, k_head_norm_w,
           conv_dw_w, conv_dw_b, conv_ln_w, conv_ln_b, attn_out_norm_w, conv_out_norm_w, w_o, ffn_norm_w,
           w_router, b_router, w_gate, b_gate, w_up, b_up, w_down, b_down):
    b, s, d = x.shape
    depth = attn_norm_w.shape[0]
    assert b == 1, "one sequence per call"
    xs = x.reshape(s, d)
    for l in range(depth):
        xs = _layer(xs, positions[0], attn_norm_w[l], w_in[l], q_a_norm_w[l], w_uq[l], kv_a_norm_w[l], w_ukv[l],
                    q_head_norm_w[l], k_head_norm_w[l], conv_dw_w[l][:, 0, :], conv_dw_b[l], conv_ln_w[l],
                    conv_ln_b[l], attn_out_norm_w[l], conv_out_norm_w[l], w_o[l], ffn_norm_w[l], w_router[l],
                    b_router[l], w_gate[l], b_gate[l], w_up[l], b_up[l], w_down[l], b_down[l])
    return xs.reshape(b, s, d)
```

```python
import functools

import jax
import jax.numpy as jnp
from jax import lax
from jax.experimental import pallas as pl
from jax.experimental.pallas import tpu as pltpu

F32, BF16, I32, U32 = jnp.float32, jnp.bfloat16, jnp.int32, jnp.uint32

N_HEADS = 16
QK_NOPE_DIM = 128
QK_ROPE_DIM = 64
QK_HEAD_DIM = QK_NOPE_DIM + QK_ROPE_DIM
V_HEAD_DIM = 128
V_AUG_DIM = V_HEAD_DIM + 16
QK_PAD_DIM = 256
ROPE_THETA = 10000.0
CONV_WIDTH = 31
N_EXPERTS = 32
TOP_K = 4
SWIGLU_LIMIT = 7.0
SWIGLU_ALPHA = 1.702
EPS = 1e-6
LOG2E = 1.4426950408889634

LANES = 128
CONV_HALO = 16
VMEM_LIMIT_BYTES = 56 * 1024 * 1024


def _cparams(*sem):
    return pltpu.CompilerParams(dimension_semantics=sem, vmem_limit_bytes=VMEM_LIMIT_BYTES)


def _rms_scale(x, n):
    return lax.rsqrt(jnp.sum(x * x, axis=-1, keepdims=True) * (1.0 / n) + EPS)


def _norm_matmul_kernel(x_ref, nw_ref, w_ref, o_ref, h_ref):
    @pl.when(pl.program_id(1) == 0)
    def _():
        x = x_ref[...]
        h_ref[...] = (x * _rms_scale(x, x.shape[-1]) * nw_ref[...]).astype(BF16)

    o_ref[...] = jnp.dot(h_ref[...], w_ref[...], preferred_element_type=F32)


def _norm_matmul(x, nw, w, *, tm, tn):
    m, k = x.shape
    n = w.shape[1]
    return pl.pallas_call(
        _norm_matmul_kernel,
        grid=(m // tm, n // tn),
        in_specs=[pl.BlockSpec((tm, k), lambda i, j: (i, 0)),
                  pl.BlockSpec((1, k), lambda i, j: (0, 0)),
                  pl.BlockSpec((k, tn), lambda i, j: (0, j))],
        out_specs=pl.BlockSpec((tm, tn), lambda i, j: (i, j)),
        out_shape=jax.ShapeDtypeStruct((m, n), F32),
        scratch_shapes=[pltpu.VMEM((tm, k), BF16)],
        compiler_params=_cparams("parallel", "arbitrary"),
        name="in_proj",
    )(x, nw, w)


def _conv_kernel(a_ref, g_ref, ap_ref, gp_ref, an_ref, gn_ref, w_ref, b_ref, lnw_ref, lnb_ref, onw_ref,
                 o_ref, uext_ref, conv_ref, *, ts, nc, rb):
    i = pl.program_id(0)
    has_prev = i > 0
    has_next = i < pl.num_programs(0) - 1
    halo = CONV_HALO

    def glu(a, g):
        return a * jax.nn.sigmoid(g)

    for c in range(nc):
        sl = slice(c * LANES, (c + 1) * LANES)
        uext_ref[c, 0:halo, :] = jnp.where(has_prev, glu(ap_ref[:, sl], gp_ref[:, sl]), 0.0)
        uext_ref[c, halo:halo + ts, :] = glu(a_ref[:, sl], g_ref[:, sl])
        uext_ref[c, halo + ts:halo + ts + halo, :] = jnp.where(has_next, glu(an_ref[:, sl], gn_ref[:, sl]), 0.0)

    off0 = halo - CONV_WIDTH // 2

    def chunk_body(c, carry):
        w = w_ref[c]
        wk = [jnp.broadcast_to(w[k:k + 1, :], (rb, LANES)) for k in range(CONV_WIDTH)]
        bias = jnp.broadcast_to(b_ref[c], (rb, LANES))
        for r in range(ts // rb):
            acc = bias
            for k in range(CONV_WIDTH):
                acc = acc + wk[k] * uext_ref[c, r * rb + off0 + k:r * rb + off0 + k + rb, :]
            conv_ref[c, r * rb:(r + 1) * rb, :] = acc
        return carry

    lax.fori_loop(0, nc, chunk_body, 0)

    n_ch = nc * LANES
    xc = conv_ref[...]
    mu = jnp.sum(jnp.sum(xc, axis=0), axis=-1, keepdims=True) * (1.0 / n_ch)
    d = xc - mu[None]
    var = jnp.sum(jnp.sum(d * d, axis=0), axis=-1, keepdims=True) * (1.0 / n_ch)
    y = d * lax.rsqrt(var + EPS)[None] * lnw_ref[...] + lnb_ref[...]
    y = y * jax.nn.sigmoid(y)
    ms = jnp.sum(jnp.sum(y * y, axis=0), axis=-1, keepdims=True) * (1.0 / n_ch)
    o = y * lax.rsqrt(ms + EPS)[None] * onw_ref[...]
    for c in range(nc):
        o_ref[:, c * LANES:(c + 1) * LANES] = o[c].astype(BF16)


def _conv_group(z, dw_w, dw_b, ln_w, ln_b, on_w, *, n_ch, ts):
    s = z.shape[0]
    nc = n_ch // LANES
    hb = ts // CONV_HALO
    n_hblk = s // CONV_HALO

    def chunked(v):
        return v.reshape(nc, 1, LANES)

    w = dw_w.reshape(CONV_WIDTH, nc, LANES).transpose(1, 0, 2)
    w = jnp.pad(w, ((0, 0), (0, 32 - CONV_WIDTH), (0, 0)))
    vec_spec = pl.BlockSpec((nc, 1, LANES), lambda i: (0, 0, 0))
    kern = functools.partial(_conv_kernel, ts=ts, nc=nc, rb=64)
    return pl.pallas_call(
        kern,
        grid=(s // ts,),
        in_specs=[pl.BlockSpec((ts, n_ch), lambda i: (i, 0)),
                  pl.BlockSpec((ts, n_ch), lambda i: (i, 1)),
                  pl.BlockSpec((CONV_HALO, n_ch), lambda i: (jnp.maximum(i * hb - 1, 0), 0)),
                  pl.BlockSpec((CONV_HALO, n_ch), lambda i: (jnp.maximum(i * hb - 1, 0), 1)),
                  pl.BlockSpec((CONV_HALO, n_ch), lambda i: (jnp.minimum((i + 1) * hb, n_hblk - 1), 0)),
                  pl.BlockSpec((CONV_HALO, n_ch), lambda i: (jnp.minimum((i + 1) * hb, n_hblk - 1), 1)),
                  pl.BlockSpec((nc, 32, LANES), lambda i: (0, 0, 0)),
                  vec_spec, vec_spec, vec_spec, vec_spec],
        out_specs=pl.BlockSpec((ts, n_ch), lambda i: (i, 0)),
        out_shape=jax.ShapeDtypeStruct((s, n_ch), BF16),
        scratch_shapes=[pltpu.VMEM((nc, ts + 2 * CONV_HALO, LANES), F32),
                        pltpu.VMEM((nc, ts, LANES), F32)],
        compiler_params=_cparams("parallel"),
        name="conv_group",
    )(z, z, z, z, z, z, w, chunked(dw_b), chunked(ln_w), chunked(ln_b), chunked(on_w))


def _q_kernel(z_ref, nw_ref, w_ref, cos_ref, sin_ref, hw_ref, q_ref, zn_ref, *, q_scale):
    @pl.when(pl.program_id(1) == 0)
    def _():
        z = z_ref[...]
        zn_ref[...] = (z * _rms_scale(z, z.shape[-1]) * nw_ref[...]).astype(BF16)

    hw = hw_ref[...]
    for hh in range(w_ref.shape[0]):
        r = jnp.dot(zn_ref[...], w_ref[hh], preferred_element_type=F32)
        nope = r[:, :LANES]
        rope = r[:, LANES:2 * LANES] * cos_ref[...] + r[:, 2 * LANES:] * sin_ref[...]
        ss = jnp.sum(nope * nope, axis=-1, keepdims=True) + jnp.sum(rope * rope, axis=-1, keepdims=True)
        inv = lax.rsqrt(ss * (1.0 / QK_HEAD_DIM) + EPS) * q_scale
        q_ref[hh, :LANES, :] = (nope * inv * hw[:, :LANES]).T.astype(BF16)
        q_ref[hh, LANES:, :] = (rope * inv * hw[:, LANES:]).T.astype(BF16)


def _q_proj(z, nw, w3, cos_t, sin_t, hw, *, col_blk, tm, hb, q_scale):
    s = z.shape[0]
    rank = w3.shape[1]
    kern = functools.partial(_q_kernel, q_scale=q_scale)
    return pl.pallas_call(
        kern,
        grid=(s // tm, N_HEADS // hb),
        in_specs=[pl.BlockSpec((tm, rank), lambda i, h: (i, col_blk)),
                  pl.BlockSpec((1, rank), lambda i, h: (0, 0)),
                  pl.BlockSpec((hb, rank, 3 * LANES), lambda i, h: (h, 0, 0)),
                  pl.BlockSpec((tm, LANES), lambda i, h: (i, 0)),
                  pl.BlockSpec((tm, LANES), lambda i, h: (i, 0)),
                  pl.BlockSpec((1, QK_PAD_DIM), lambda i, h: (0, 0))],
        out_specs=pl.BlockSpec((hb, QK_PAD_DIM, tm), lambda i, h: (h, 0, i)),
        out_shape=jax.ShapeDtypeStruct((N_HEADS, QK_PAD_DIM, s), BF16),
        scratch_shapes=[pltpu.VMEM((tm, rank), BF16)],
        compiler_params=_cparams("parallel", "arbitrary"),
        name="q_proj",
    )(z, nw, w3, cos_t, sin_t, hw)


def _kv_kernel(z_ref, kpe_ref, kpes_ref, nw_ref, w_ref, cos_ref, sin_ref, hw_ref, k_ref, v_ref,
               zn_ref, kr_ref, ss_ref):
    @pl.when(pl.program_id(1) == 0)
    def _():
        z = z_ref[...]
        zn_ref[...] = (z * _rms_scale(z, z.shape[-1]) * nw_ref[...]).astype(BF16)
        kr = kpe_ref[...] * cos_ref[...] + kpes_ref[...] * sin_ref[...]
        kr_ref[...] = kr
        ss_ref[...] = jnp.sum(kr * kr, axis=-1, keepdims=True)

    hw = hw_ref[...]
    for hh in range(w_ref.shape[0]):
        r = jnp.dot(zn_ref[...], w_ref[hh], preferred_element_type=F32)
        kn = r[:, :LANES]
        ss = jnp.sum(kn * kn, axis=-1, keepdims=True) + ss_ref[...]
        inv = lax.rsqrt(ss * (1.0 / QK_HEAD_DIM) + EPS)
        k_ref[hh, :, :LANES] = (kn * inv * hw[:, :LANES]).astype(BF16)
        k_ref[hh, :, LANES:] = (kr_ref[...] * inv * hw[:, LANES:]).astype(BF16)
        v_ref[hh, :V_HEAD_DIM, :] = r[:, LANES:].T.astype(BF16)
        ones_row = lax.broadcasted_iota(I32, (V_AUG_DIM - V_HEAD_DIM, r.shape[0]), 0) == 0
        v_ref[hh, V_HEAD_DIM:, :] = jnp.where(ones_row, 1.0, 0.0).astype(BF16)


def _kv_proj(z, nw, w3, cos_t, sin_t, hw, *, ckv_blk, kpe_blk, tm, hb):
    s = z.shape[0]
    rank = w3.shape[1]
    return pl.pallas_call(
        _kv_kernel,
        grid=(s // tm, N_HEADS // hb),
        in_specs=[pl.BlockSpec((tm, rank), lambda i, h: (i, ckv_blk)),
                  pl.BlockSpec((tm, LANES), lambda i, h: (i, kpe_blk)),
                  pl.BlockSpec((tm, LANES), lambda i, h: (i, kpe_blk + 1)),
                  pl.BlockSpec((1, rank), lambda i, h: (0, 0)),
                  pl.BlockSpec((hb, rank, 2 * LANES), lambda i, h: (h, 0, 0)),
                  pl.BlockSpec((tm, LANES), lambda i, h: (i, 0)),
                  pl.BlockSpec((tm, LANES), lambda i, h: (i, 0)),
                  pl.BlockSpec((1, QK_PAD_DIM), lambda i, h: (0, 0))],
        out_specs=[pl.BlockSpec((hb, tm, QK_PAD_DIM), lambda i, h: (h, i, 0)),
                   pl.BlockSpec((hb, V_AUG_DIM, tm), lambda i, h: (h, 0, i))],
        out_shape=[jax.ShapeDtypeStruct((N_HEADS, s, QK_PAD_DIM), BF16),
                   jax.ShapeDtypeStruct((N_HEADS, V_AUG_DIM, s), BF16)],
        scratch_shapes=[pltpu.VMEM((tm, rank), BF16),
                        pltpu.VMEM((tm, LANES), F32),
                        pltpu.VMEM((tm, 1), F32)],
        compiler_params=_cparams("parallel", "arbitrary"),
        name="kv_proj",
    )(z, z, z, nw, w3, cos_t, sin_t, hw)


def _attn_kernel(qt_ref, k_ref, vt_ref, o_ref, st_ref, *, tk):
    qt = qt_ref[0]
    tq = qt.shape[1]
    nk = k_ref.shape[1] // tk

    def scores(j, slot):
        start = pl.multiple_of(j * tk, tk)
        st_ref[slot] = jnp.dot(k_ref[0, pl.ds(start, tk), :], qt, preferred_element_type=F32)

    def update(j, slot, carry):
        m, acc = carry
        start = pl.multiple_of(j * tk, tk)
        vts = vt_ref[0, :, pl.ds(start, tk)]
        st = st_ref[slot]
        m_new = jnp.maximum(m, jnp.max(st, axis=0, keepdims=True))
        p = jnp.exp2(st - m_new)
        alpha = jnp.exp2(m - m_new)
        acc = alpha * acc + jnp.dot(vts, p.astype(BF16), preferred_element_type=F32)
        return m_new, acc

    def body(jj, carry):
        j = 2 * jj
        scores(j + 1, 1)
        carry = update(j, 0, carry)
        scores(jnp.minimum(j + 2, nk - 1), 0)
        return update(j + 1, 1, carry)

    m0 = jnp.full((1, tq), -jnp.inf, F32)
    acc0 = jnp.zeros((V_AUG_DIM, tq), F32)
    scores(0, 0)
    _, acc = lax.fori_loop(0, nk // 2, body, (m0, acc0), unroll=4)
    o_ref[...] = (acc[:V_HEAD_DIM] / acc[V_HEAD_DIM:V_HEAD_DIM + 1]).T


def _attention(qt, k, vt, *, tq, tk):
    h, s, _ = k.shape
    kern = functools.partial(_attn_kernel, tk=tk)
    return pl.pallas_call(
        kern,
        grid=(h, s // tq),
        in_specs=[pl.BlockSpec((1, QK_PAD_DIM, tq), lambda hh, i: (hh, 0, i)),
                  pl.BlockSpec((1, s, QK_PAD_DIM), lambda hh, i: (hh, 0, 0)),
                  pl.BlockSpec((1, V_AUG_DIM, s), lambda hh, i: (hh, 0, 0))],
        out_specs=pl.BlockSpec((tq, V_HEAD_DIM), lambda hh, i: (i, hh)),
        out_shape=jax.ShapeDtypeStruct((s, h * V_HEAD_DIM), F32),
        scratch_shapes=[pltpu.VMEM((2, tk, tq), F32)],
        compiler_params=_cparams("parallel", "arbitrary"),
        name="attention",
    )(qt, k, vt)


def _out_proj_kernel(a_ref, c_ref, x_ref, nw_ref, wa_ref, wc_ref, o_ref, an_ref):
    @pl.when(pl.program_id(1) == 0)
    def _():
        a = a_ref[...]
        an_ref[...] = (a * _rms_scale(a, a.shape[-1]) * nw_ref[...]).astype(BF16)

    acc = jnp.dot(an_ref[...], wa_ref[...], preferred_element_type=F32)
    acc = acc + jnp.dot(c_ref[...], wc_ref[...], preferred_element_type=F32)
    o_ref[...] = x_ref[...] + acc


def _out_proj(attn, conv_n, x, nw, w_o, *, tm, tn):
    s, da = attn.shape
    dc = conv_n.shape[1]
    d = x.shape[1]
    assert da == dc, "the two head groups share one W_o row-block size"
    return pl.pallas_call(
        _out_proj_kernel,
        grid=(s // tm, d // tn),
        in_specs=[pl.BlockSpec((tm, da), lambda i, j: (i, 0)),
                  pl.BlockSpec((tm, dc), lambda i, j: (i, 0)),
                  pl.BlockSpec((tm, tn), lambda i, j: (i, j)),
                  pl.BlockSpec((1, da), lambda i, j: (0, 0)),
                  pl.BlockSpec((da, tn), lambda i, j: (0, j)),
                  pl.BlockSpec((dc, tn), lambda i, j: (1, j))],
        out_specs=pl.BlockSpec((tm, tn), lambda i, j: (i, j)),
        out_shape=jax.ShapeDtypeStruct((s, d), F32),
        scratch_shapes=[pltpu.VMEM((tm, da), BF16)],
        compiler_params=_cparams("parallel", "arbitrary"),
        name="out_proj",
    )(attn, conv_n, x, nw, w_o, w_o)


def _router_kernel(x_ref, nw_ref, wr_ref, br_ref, tri_ref, h_ref, idx_ref, gate_ref, rank_ref, cnt_ref, run_ref):
    @pl.when(pl.program_id(0) == 0)
    def _():
        run_ref[...] = jnp.zeros_like(run_ref)

    x = x_ref[...]
    h = x * _rms_scale(x, x.shape[-1]) * nw_ref[...]
    half = h.shape[1] // 2
    lo = lax.bitcast_convert_type(h[:, :half].astype(BF16).astype(F32), U32)
    hi = lax.bitcast_convert_type(h[:, half:].astype(BF16).astype(F32), U32)
    h_ref[...] = (lo >> 16) | hi
    tm = x.shape[0]
    logits = lax.dot_general(wr_ref[...], h, (((1,), (1,)), ((), ())), preferred_element_type=F32,
                             precision=lax.Precision.HIGHEST) + br_ref[...]
    eio = lax.broadcasted_iota(I32, (N_EXPERTS, tm), 0)
    vals, idxs = [], []
    cur = logits
    for _ in range(TOP_K):
        m = jnp.max(cur, axis=0, keepdims=True)
        idx = jnp.min(jnp.where(cur == m, eio, N_EXPERTS), axis=0, keepdims=True)
        vals.append(m)
        idxs.append(idx)
        cur = jnp.where(eio == idx, -jnp.inf, cur)
    es = [jnp.exp(v - vals[0]) for v in vals]
    den = es[0] + es[1] + es[2] + es[3]
    base = run_ref[...]
    for k in range(TOP_K):
        gate_ref[k:k + 1, :] = es[k] / den
        idx_ref[k:k + 1, :] = idxs[k]
        onehot = eio == idxs[k]
        before = jnp.dot(onehot.astype(BF16), tri_ref[...], preferred_element_type=F32)
        rank = jnp.sum(jnp.where(onehot, before + base, 0.0), axis=0, keepdims=True)
        rank_ref[k:k + 1, :] = rank.astype(I32)
        base = base + jnp.sum(onehot.astype(F32), axis=1, keepdims=True)
    run_ref[...] = base
    cnt_ref[...] = base.astype(I32)


def _router(x1, nw, wr_t, br, *, tm):
    t, d = x1.shape
    tri = (lax.broadcasted_iota(I32, (tm, tm), 0) < lax.broadcasted_iota(I32, (tm, tm), 1)).astype(BF16)
    row4 = pl.BlockSpec((TOP_K, tm), lambda i: (0, i))
    return pl.pallas_call(
        _router_kernel,
        grid=(t // tm,),
        in_specs=[pl.BlockSpec((tm, d), lambda i: (i, 0)),
                  pl.BlockSpec((1, d), lambda i: (0, 0)),
                  pl.BlockSpec((N_EXPERTS, d), lambda i: (0, 0)),
                  pl.BlockSpec((N_EXPERTS, 1), lambda i: (0, 0)),
                  pl.BlockSpec((tm, tm), lambda i: (0, 0))],
        out_specs=[pl.BlockSpec((tm, d // 2), lambda i: (i, 0)), row4, row4, row4,
                   pl.BlockSpec((N_EXPERTS, 1), lambda i: (0, 0))],
        out_shape=[jax.ShapeDtypeStruct((t, d // 2), U32),
                   jax.ShapeDtypeStruct((TOP_K, t), I32),
                   jax.ShapeDtypeStruct((TOP_K, t), F32),
                   jax.ShapeDtypeStruct((TOP_K, t), I32),
                   jax.ShapeDtypeStruct((N_EXPERTS, 1), I32)],
        scratch_shapes=[pltpu.VMEM((N_EXPERTS, 1), F32)],
        compiler_params=_cparams("arbitrary"),
        name="router",
    )(x1, nw, wr_t, br, tri)


def _gather_kernel(zflag_ref, dest_ref, h_ref, xs_hbm, zero_ref, sem, zsem, *, tt, tm):
    n_tiles = xs_hbm.shape[0] // tm

    @pl.when(pl.program_id(0) == 0)
    def _():
        zero_ref[...] = jnp.zeros_like(zero_ref)

        def zero_copy(g):
            return pltpu.make_async_copy(zero_ref, xs_hbm.at[pl.ds(pl.multiple_of(g * tm, tm), tm)], zsem)

        def zstart(g, carry):
            @pl.when(zflag_ref[g] != 0)
            def _():
                zero_copy(g).start()
            return carry

        def zwait(g, carry):
            @pl.when(zflag_ref[g] != 0)
            def _():
                zero_copy(g).wait()
            return carry

        lax.fori_loop(0, n_tiles, zstart, 0)
        lax.fori_loop(0, n_tiles, zwait, 0)

    def row_copy(src_row, dst_row):
        return pltpu.make_async_copy(h_ref.at[pl.ds(src_row, 1)], xs_hbm.at[pl.ds(dst_row, 1)], sem)

    def start_body(t, carry):
        for k in range(TOP_K):
            row_copy(t, dest_ref[0, k, t]).start()
        return carry

    def wait_body(t, carry):
        for k in range(TOP_K):
            row_copy(0, 0).wait()
        return carry

    lax.fori_loop(0, tt, start_body, 0)
    lax.fori_loop(0, tt, wait_body, 0)


def _moe_gather(zflag, h2, dest, n_rows, *, tt, tm):
    t, d = h2.shape
    dest3 = dest.reshape(TOP_K, t // tt, tt).transpose(1, 0, 2)
    kern = functools.partial(_gather_kernel, tt=tt, tm=tm)
    return pl.pallas_call(
        kern,
        grid_spec=pltpu.PrefetchScalarGridSpec(
            num_scalar_prefetch=1,
            grid=(t // tt,),
            in_specs=[pl.BlockSpec((1, TOP_K, tt), lambda i, zf: (i, 0, 0), memory_space=pltpu.SMEM),
                      pl.BlockSpec((tt, d), lambda i, zf: (i, 0))],
            out_specs=pl.BlockSpec(memory_space=pl.ANY),
            scratch_shapes=[pltpu.VMEM((tm, d), h2.dtype), pltpu.SemaphoreType.DMA(()),
                            pltpu.SemaphoreType.DMA(())],
        ),
        out_shape=jax.ShapeDtypeStruct((n_rows, d), h2.dtype),
        compiler_params=_cparams("arbitrary"),
        name="moe_gather",
    )(zflag, dest3, h2)


def _expert_row_tiles(ts_ref, nt_ref, nv_ref, cnt_ref, src_hbm, dst_hbm, in_buf, out_buf, zero_ref, in_sem, out_sem, zsem,
                      compute):
    j, e = pl.program_id(0), pl.program_id(1)
    nj, ne = pl.num_programs(0), pl.num_programs(1)
    tm, tn = out_buf.shape[1], out_buf.shape[2]
    n_tiles = dst_hbm.shape[0] // tm
    nv = nv_ref[0]
    t0 = ts_ref[e]
    col = pl.multiple_of(j * tn, tn)

    def rows(g):
        return pl.ds(pl.multiple_of(g * tm, tm), tm)

    def in_copy(g, slot):
        return pltpu.make_async_copy(src_hbm.at[rows(g)], in_buf.at[slot], in_sem.at[slot])

    def out_copy(g, slot):
        return pltpu.make_async_copy(out_buf.at[slot], dst_hbm.at[rows(g), pl.ds(col, tn)], out_sem.at[slot])

    def zero_copy(g):
        return pltpu.make_async_copy(zero_ref, dst_hbm.at[rows(g), pl.ds(col, tn)], zsem)

    @pl.when(jnp.logical_and(j == 0, e == 0))
    def _():
        zero_ref[...] = jnp.zeros_like(zero_ref)

    def tile_body(t, carry):
        g = t0 + t
        seq = j * nv + g
        slot = seq % 2

        @pl.when(seq == 0)
        def _():
            in_copy(g, slot).start()

        in_copy(g, slot).wait()
        last_of_sweep = g + 1 == nv

        @pl.when(jnp.logical_or(jnp.logical_not(last_of_sweep), j + 1 < nj))
        def _():
            in_copy(jnp.where(last_of_sweep, 0, g + 1), 1 - slot).start()

        @pl.when(seq >= 2)
        def _():
            out_copy(g, slot).wait()

        half_rows = tm // 2
        few_rows = cnt_ref[e] - t * tm <= half_rows

        @pl.when(few_rows)
        def _():
            out_buf[slot, :half_rows] = compute(in_buf[slot, :half_rows])
            out_buf[slot, half_rows:] = jnp.zeros((tm - half_rows, tn), out_buf.dtype)

        @pl.when(jnp.logical_not(few_rows))
        def _():
            out_buf[slot] = compute(in_buf[slot])

        out_copy(g, slot).start()
        return carry

    lax.fori_loop(0, nt_ref[e], tile_body, 0)

    @pl.when(e == ne - 1)
    def _():
        def zstart(g, carry):
            zero_copy(g).start()
            return carry

        def zwait(g, carry):
            zero_copy(g).wait()
            return carry

        lax.fori_loop(nv, n_tiles, zstart, 0)
        lax.fori_loop(nv, n_tiles, zwait, 0)

    @pl.when(jnp.logical_and(e == ne - 1, j == nj - 1))
    def _():
        total = nj * nv

        @pl.when(total >= 1)
        def _():
            out_copy(0, (total - 1) % 2).wait()

        @pl.when(total >= 2)
        def _():
            out_copy(0, total % 2).wait()


def _expert_up_kernel(ts_ref, nt_ref, nv_ref, cnt_ref, wg_ref, bg_ref, wu_ref, bu_ref, xs_hbm, act_hbm,
                      wgc_ref, wuc_ref, in_buf, out_buf, zero_ref, in_sem, out_sem, zsem):
    @pl.when(nt_ref[pl.program_id(1)] > 0)
    def _():
        wgc_ref[...] = wg_ref[0].astype(BF16)
        wuc_ref[...] = wu_ref[0].astype(BF16)

    def compute(xw):
        half = xw.shape[1]
        x_lo = lax.bitcast_convert_type(xw << 16, F32).astype(BF16)
        x_hi = lax.bitcast_convert_type((xw >> 16) << 16, F32).astype(BF16)

        def proj(w_ref, b_ref):
            y = jnp.dot(x_lo, w_ref[:half, :], preferred_element_type=F32)
            return y + jnp.dot(x_hi, w_ref[half:, :], preferred_element_type=F32) + b_ref[0]

        gate = proj(wgc_ref, bg_ref)
        lin = proj(wuc_ref, bu_ref)
        gate = jnp.minimum(gate, SWIGLU_LIMIT)
        lin = jnp.clip(lin, -SWIGLU_LIMIT, SWIGLU_LIMIT)
        return (gate * jax.nn.sigmoid(SWIGLU_ALPHA * gate) * (lin + 1.0)).astype(BF16)

    _expert_row_tiles(ts_ref, nt_ref, nv_ref, cnt_ref, xs_hbm, act_hbm, in_buf, out_buf, zero_ref, in_sem, out_sem, zsem,
                      compute)


def _w_blk(j, e, ts, nt, nv, cnt):
    return (e, 0, j)


def _row_tile_scratch(tm, k_in, in_dtype, tn, out_dtype):
    return [pltpu.VMEM((2, tm, k_in), in_dtype), pltpu.VMEM((2, tm, tn), out_dtype), pltpu.VMEM((tm, tn), out_dtype),
            pltpu.SemaphoreType.DMA((2,)), pltpu.SemaphoreType.DMA((2,)), pltpu.SemaphoreType.DMA(())]


def _expert_up(plan, xs, wg, bg, wu, bu, *, tm, tn):
    r, dw = xs.shape
    n_e, d, de = wg.shape
    assert 2 * dw == d, "xs rows hold two bf16 columns per 32-bit word"
    return pl.pallas_call(
        _expert_up_kernel,
        grid_spec=pltpu.PrefetchScalarGridSpec(
            num_scalar_prefetch=4,
            grid=(de // tn, n_e),
            in_specs=[pl.BlockSpec((1, d, tn), _w_blk),
                      pl.BlockSpec((1, 1, tn), _w_blk),
                      pl.BlockSpec((1, d, tn), _w_blk),
                      pl.BlockSpec((1, 1, tn), _w_blk),
                      pl.BlockSpec(memory_space=pl.ANY)],
            out_specs=pl.BlockSpec(memory_space=pl.ANY),
            scratch_shapes=[pltpu.VMEM((d, tn), BF16), pltpu.VMEM((d, tn), BF16)]
            + _row_tile_scratch(tm, dw, xs.dtype, tn, BF16),
        ),
        out_shape=jax.ShapeDtypeStruct((r, de), BF16),
        compiler_params=_cparams("arbitrary", "arbitrary"),
        name="expert_up",
    )(*plan, wg, bg, wu, bu, xs)


def _expert_down_kernel(ts_ref, nt_ref, nv_ref, cnt_ref, w_ref, b_ref, act_hbm, ys_hbm,
                        wc_ref, in_buf, out_buf, zero_ref, in_sem, out_sem, zsem):
    @pl.when(nt_ref[pl.program_id(1)] > 0)
    def _():
        wc_ref[...] = w_ref[0].astype(BF16)

    def compute(a):
        return jnp.dot(a, wc_ref[...], preferred_element_type=F32) + b_ref[0]

    _expert_row_tiles(ts_ref, nt_ref, nv_ref, cnt_ref, act_hbm, ys_hbm, in_buf, out_buf, zero_ref, in_sem, out_sem, zsem,
                      compute)


def _expert_down(plan, act, wd, bd, *, tm, tn):
    r, de = act.shape
    n_e, _, d = wd.shape
    return pl.pallas_call(
        _expert_down_kernel,
        grid_spec=pltpu.PrefetchScalarGridSpec(
            num_scalar_prefetch=4,
            grid=(d // tn, n_e),
            in_specs=[pl.BlockSpec((1, de, tn), _w_blk),
                      pl.BlockSpec((1, 1, tn), _w_blk),
                      pl.BlockSpec(memory_space=pl.ANY)],
            out_specs=pl.BlockSpec(memory_space=pl.ANY),
            scratch_shapes=[pltpu.VMEM((de, tn), BF16)] + _row_tile_scratch(tm, de, act.dtype, tn, F32),
        ),
        out_shape=jax.ShapeDtypeStruct((r, d), F32),
        compiler_params=_cparams("arbitrary", "arbitrary"),
        name="expert_down",
    )(*plan, wd, bd, act)


def _combine_kernel(dest_ref, dest_next_ref, x_ref, g_ref, ys_hbm, o_ref, buf_ref, sem, *, tt):
    i = pl.program_id(0)
    slot = i % 2

    def row_copy(src_row, s, k, t):
        return pltpu.make_async_copy(ys_hbm.at[pl.ds(src_row, 1)], buf_ref.at[s, k, pl.ds(t, 1)], sem.at[s])

    def fetch(d_ref, s):
        def start_body(t, carry):
            for k in range(TOP_K):
                row_copy(d_ref[0, k, t], s, k, t).start()
            return carry

        lax.fori_loop(0, tt, start_body, 0)

    @pl.when(i == 0)
    def _():
        fetch(dest_ref, slot)

    @pl.when(i + 1 < pl.num_programs(0))
    def _():
        fetch(dest_next_ref, 1 - slot)

    def wait_body(t, carry):
        for k in range(TOP_K):
            row_copy(0, slot, k, t).wait()
        return carry

    lax.fori_loop(0, tt, wait_body, 0)
    g = g_ref[...]
    acc = x_ref[...]
    for k in range(TOP_K):
        acc = acc + g[:, k:k + 1] * buf_ref[slot, k]
    o_ref[...] = acc


def _moe_combine(dest, x1, gates_t, ys, *, tt):
    t, d = x1.shape
    n_steps = t // tt
    dest3 = dest.reshape(TOP_K, n_steps, tt).transpose(1, 0, 2)
    kern = functools.partial(_combine_kernel, tt=tt)
    return pl.pallas_call(
        kern,
        grid=(n_steps,),
        in_specs=[pl.BlockSpec((1, TOP_K, tt), lambda i: (i, 0, 0), memory_space=pltpu.SMEM),
                  pl.BlockSpec((1, TOP_K, tt), lambda i: (jnp.minimum(i + 1, n_steps - 1), 0, 0),
                               memory_space=pltpu.SMEM),
                  pl.BlockSpec((tt, d), lambda i: (i, 0)),
                  pl.BlockSpec((tt, TOP_K), lambda i: (i, 0)),
                  pl.BlockSpec(memory_space=pl.ANY)],
        out_specs=pl.BlockSpec((tt, d), lambda i: (i, 0)),
        out_shape=jax.ShapeDtypeStruct((t, d), F32),
        scratch_shapes=[pltpu.VMEM((2, TOP_K, tt, d), F32), pltpu.SemaphoreType.DMA((2,))],
        compiler_params=_cparams("arbitrary"),
        name="moe_combine",
    )(dest3, dest3, x1, gates_t, ys)


def _rope_tables(positions):
    half = QK_ROPE_DIM // 2
    inv_freq = ROPE_THETA ** (-jnp.arange(0, QK_ROPE_DIM, 2, dtype=F32) / QK_ROPE_DIM)
    ang = positions.astype(F32)[:, None] * inv_freq
    cos, sin = jnp.cos(ang), jnp.sin(ang)
    zeros = jnp.zeros((positions.shape[0], LANES - 2 * half), F32)
    return jnp.concatenate([cos, cos, zeros], axis=1), jnp.concatenate([-sin, sin, zeros], axis=1)


def _swap_halves(w):
    half = w.shape[-1] // 2
    return jnp.concatenate([w[..., half:], w[..., :half]], axis=-1)


def _layer(x, positions, attn_norm_w, w_in, q_a_norm_w, w_uq, kv_a_norm_w, w_ukv, q_head_norm_w, k_head_norm_w,
           conv_dw_w, conv_dw_b, conv_ln_w, conv_ln_b, attn_out_norm_w, conv_out_norm_w, w_o, ffn_norm_w,
           w_router, b_router, w_gate, b_gate, w_up, b_up, w_down, b_down):
    s, d = x.shape
    q_rank = q_a_norm_w.shape[0]
    kv_rank = kv_a_norm_w.shape[0]
    n_conv = conv_dw_b.shape[0]
    glu_w = 2 * n_conv
    de = w_gate.shape[2]

    pad64 = jnp.zeros((d, LANES - QK_ROPE_DIM), F32)
    w_kpe = w_in[:, glu_w + q_rank + kv_rank:]
    z_cols = glu_w + q_rank + kv_rank + 2 * LANES
    in_tn = 1024
    z_pad = -z_cols % in_tn
    w_in_p = jnp.concatenate(
        [w_in[:, :glu_w + q_rank + kv_rank], w_kpe, pad64, _swap_halves(w_kpe), pad64, jnp.zeros((d, z_pad), F32)],
        axis=1).astype(BF16)

    wq = w_uq.reshape(q_rank, N_HEADS, QK_HEAD_DIM)
    wq_pe = wq[:, :, QK_NOPE_DIM:]
    zq = jnp.zeros((q_rank, N_HEADS, LANES - QK_ROPE_DIM), F32)
    wq3 = jnp.concatenate([wq[:, :, :QK_NOPE_DIM], wq_pe, zq, _swap_halves(wq_pe), zq], axis=2)
    wq3 = wq3.transpose(1, 0, 2).astype(BF16)
    wkv3 = w_ukv.reshape(kv_rank, N_HEADS, QK_NOPE_DIM + V_HEAD_DIM).transpose(1, 0, 2).astype(BF16)
    zero_hw = jnp.zeros((QK_PAD_DIM - QK_HEAD_DIM,), F32)
    q_hw = jnp.concatenate([q_head_norm_w, zero_hw])[None]
    k_hw = jnp.concatenate([k_head_norm_w, zero_hw])[None]
    cos_t, sin_t = _rope_tables(positions)

    z = _norm_matmul(x, attn_norm_w[None], w_in_p, tm=512, tn=in_tn)
    conv_n = _conv_group(z, conv_dw_w, conv_dw_b, conv_ln_w, conv_ln_b, conv_out_norm_w, n_ch=n_conv, ts=256)
    q_scale = QK_HEAD_DIM ** -0.5 * LOG2E
    q = _q_proj(z, q_a_norm_w[None], wq3, cos_t, sin_t, q_hw, col_blk=glu_w // q_rank, tm=512, hb=8,
                q_scale=q_scale)
    k, v = _kv_proj(z, kv_a_norm_w[None], wkv3, cos_t, sin_t, k_hw, ckv_blk=(glu_w + q_rank) // kv_rank,
                    kpe_blk=(glu_w + q_rank + kv_rank) // LANES, tm=512, hb=8)
    attn = _attention(q, k, v, tq=512, tk=512)
    x1 = _out_proj(attn, conv_n, x, attn_out_norm_w[None], w_o.astype(BF16), tm=512, tn=1024)

    tm_e = 256
    h2, top_idx, gates, rank, counts = _router(x1, ffn_norm_w[None], w_router.T, b_router[:, None], tm=512)
    counts = counts[:, 0]
    tiles_per_e = (counts + tm_e - 1) // tm_e
    tile_end = jnp.cumsum(tiles_per_e)
    tile_start = tile_end - tiles_per_e
    n_valid = tile_end[-1:]
    expert_ids = jnp.arange(N_EXPERTS, dtype=I32)[:, None, None]
    row_start = (tile_start * tm_e)[:, None, None]
    dest = jnp.sum(jnp.where(top_idx[None] == expert_ids, row_start, 0), axis=0) + rank
    n_rows = s * TOP_K + N_EXPERTS * tm_e
    tile_ids = jnp.arange(n_rows // tm_e, dtype=I32)
    has_pad_rows = jnp.any((tile_ids[:, None] == tile_end[None, :] - 1) & (counts % tm_e != 0)[None, :], axis=1)
    zflag = (has_pad_rows | (tile_ids >= n_valid[0])).astype(I32)

    xs = _moe_gather(zflag, h2, dest, n_rows, tt=512, tm=tm_e)
    plan = (tile_start, tiles_per_e, n_valid, counts)
    act = _expert_up(plan, xs, w_gate, b_gate[:, None, :], w_up, b_up[:, None, :], tm=tm_e, tn=512)
    ys = _expert_down(plan, act, w_down, b_down[:, None, :], tm=tm_e, tn=2048)
    return _moe_combine(dest, x1, gates.T, ys, tt=128)


def kernel(x, positions, attn_norm_w, w_in, q_a_norm_w, w_uq, kv_a_norm_w, w_ukv, q_head_norm_w, k_head_norm_w,
           conv_dw_w, conv_dw_b, conv_ln_w, conv_ln_b, attn_out_norm_w, conv_out_norm_w, w_o, ffn_norm_w,
           w_router, b_router, w_gate, b_gate, w_up, b_up, w_down, b_down):
    b, s, d = x.shape
    depth = attn_norm_w.shape[0]
    assert b == 1, "one sequence per call"
    xs = x.reshape(s, d)
    for l in range(depth):
        xs = _layer(xs, positions[0], attn_norm_w[l], w_in[l], q_a_norm_w[l], w_uq[l], kv_a_norm_w[l], w_ukv[l],
                    q_head_norm_w[l], k_head_norm_w[l], conv_dw_w[l][:, 0, :], conv_dw_b[l], conv_ln_w[l],
                    conv_ln_b[l], attn_out_norm_w[l], conv_out_norm_w[l], w_o[l], ffn_norm_w[l], w_router[l],
                    b_router[l], w_gate[l], b_gate[l], w_up[l], b_up[l], w_down[l], b_down[l])
    return xs.reshape(b, s, d)
```

```python
import functools

import jax
import jax.numpy as jnp
from jax import lax
from jax.experimental import pallas as pl
from jax.experimental.pallas import tpu as pltpu

F32, BF16, I32, U32 = jnp.float32, jnp.bfloat16, jnp.int32, jnp.uint32

N_HEADS = 16
QK_NOPE_DIM = 128
QK_ROPE_DIM = 64
QK_HEAD_DIM = QK_NOPE_DIM + QK_ROPE_DIM
V_HEAD_DIM = 128
V_AUG_DIM = V_HEAD_DIM + 16
QK_PAD_DIM = 256
ROPE_THETA = 10000.0
CONV_WIDTH = 31
N_EXPERTS = 32
TOP_K = 4
SWIGLU_LIMIT = 7.0
SWIGLU_ALPHA = 1.702
EPS = 1e-6
LOG2E = 1.4426950408889634

LANES = 128
CONV_HALO = 16
VMEM_LIMIT_BYTES = 56 * 1024 * 1024


def _cparams(*sem):
    return pltpu.CompilerParams(dimension_semantics=sem, vmem_limit_bytes=VMEM_LIMIT_BYTES)


def _rms_scale(x, n):
    return lax.rsqrt(jnp.sum(x * x, axis=-1, keepdims=True) * (1.0 / n) + EPS)


def _norm_matmul_kernel(x_ref, nw_ref, w_ref, o_ref, h_ref):
    @pl.when(pl.program_id(1) == 0)
    def _():
        x = x_ref[...]
        h_ref[...] = (x * _rms_scale(x, x.shape[-1]) * nw_ref[...]).astype(BF16)

    o_ref[...] = jnp.dot(h_ref[...], w_ref[...], preferred_element_type=F32)


def _norm_matmul(x, nw, w, *, tm, tn):
    m, k = x.shape
    n = w.shape[1]
    return pl.pallas_call(
        _norm_matmul_kernel,
        grid=(m // tm, n // tn),
        in_specs=[pl.BlockSpec((tm, k), lambda i, j: (i, 0)),
                  pl.BlockSpec((1, k), lambda i, j: (0, 0)),
                  pl.BlockSpec((k, tn), lambda i, j: (0, j))],
        out_specs=pl.BlockSpec((tm, tn), lambda i, j: (i, j)),
        out_shape=jax.ShapeDtypeStruct((m, n), F32),
        scratch_shapes=[pltpu.VMEM((tm, k), BF16)],
        compiler_params=_cparams("parallel", "arbitrary"),
        name="in_proj",
    )(x, nw, w)


def _conv_kernel(a_ref, g_ref, ap_ref, gp_ref, an_ref, gn_ref, w_ref, b_ref, lnw_ref, lnb_ref, onw_ref,
                 o_ref, uext_ref, conv_ref, *, ts, nc, rb):
    i = pl.program_id(0)
    has_prev = i > 0
    has_next = i < pl.num_programs(0) - 1
    halo = CONV_HALO

    def glu(a, g):
        return a * jax.nn.sigmoid(g)

    for c in range(nc):
        sl = slice(c * LANES, (c + 1) * LANES)
        uext_ref[c, 0:halo, :] = jnp.where(has_prev, glu(ap_ref[:, sl], gp_ref[:, sl]), 0.0)
        uext_ref[c, halo:halo + ts, :] = glu(a_ref[:, sl], g_ref[:, sl])
        uext_ref[c, halo + ts:halo + ts + halo, :] = jnp.where(has_next, glu(an_ref[:, sl], gn_ref[:, sl]), 0.0)

    off0 = halo - CONV_WIDTH // 2

    def chunk_body(c, carry):
        w = w_ref[c]
        wk = [jnp.broadcast_to(w[k:k + 1, :], (rb, LANES)) for k in range(CONV_WIDTH)]
        bias = jnp.broadcast_to(b_ref[c], (rb, LANES))
        for r in range(ts // rb):
            acc = bias
            for k in range(CONV_WIDTH):
                acc = acc + wk[k] * uext_ref[c, r * rb + off0 + k:r * rb + off0 + k + rb, :]
            conv_ref[c, r * rb:(r + 1) * rb, :] = acc
        return carry

    lax.fori_loop(0, nc, chunk_body, 0)

    n_ch = nc * LANES
    xc = conv_ref[...]
    mu = jnp.sum(jnp.sum(xc, axis=0), axis=-1, keepdims=True) * (1.0 / n_ch)
    d = xc - mu[None]
    var = jnp.sum(jnp.sum(d * d, axis=0), axis=-1, keepdims=True) * (1.0 / n_ch)
    y = d * lax.rsqrt(var + EPS)[None] * lnw_ref[...] + lnb_ref[...]
    y = y * jax.nn.sigmoid(y)
    ms = jnp.sum(jnp.sum(y * y, axis=0), axis=-1, keepdims=True) * (1.0 / n_ch)
    o = y * lax.rsqrt(ms + EPS)[None] * onw_ref[...]
    for c in range(nc):
        o_ref[:, c * LANES:(c + 1) * LANES] = o[c].astype(BF16)


def _conv_group(z, dw_w, dw_b, ln_w, ln_b, on_w, *, n_ch, ts):
    s = z.shape[0]
    nc = n_ch // LANES
    hb = ts // CONV_HALO
    n_hblk = s // CONV_HALO

    def chunked(v):
        return v.reshape(nc, 1, LANES)

    w = dw_w.reshape(CONV_WIDTH, nc, LANES).transpose(1, 0, 2)
    w = jnp.pad(w, ((0, 0), (0, 32 - CONV_WIDTH), (0, 0)))
    vec_spec = pl.BlockSpec((nc, 1, LANES), lambda i: (0, 0, 0))
    kern = functools.partial(_conv_kernel, ts=ts, nc=nc, rb=64)
    return pl.pallas_call(
        kern,
        grid=(s // ts,),
        in_specs=[pl.BlockSpec((ts, n_ch), lambda i: (i, 0)),
                  pl.BlockSpec((ts, n_ch), lambda i: (i, 1)),
                  pl.BlockSpec((CONV_HALO, n_ch), lambda i: (jnp.maximum(i * hb - 1, 0), 0)),
                  pl.BlockSpec((CONV_HALO, n_ch), lambda i: (jnp.maximum(i * hb - 1, 0), 1)),
                  pl.BlockSpec((CONV_HALO, n_ch), lambda i: (jnp.minimum((i + 1) * hb, n_hblk - 1), 0)),
                  pl.BlockSpec((CONV_HALO, n_ch), lambda i: (jnp.minimum((i + 1) * hb, n_hblk - 1), 1)),
                  pl.BlockSpec((nc, 32, LANES), lambda i: (0, 0, 0)),
                  vec_spec, vec_spec, vec_spec, vec_spec],
        out_specs=pl.BlockSpec((ts, n_ch), lambda i: (i, 0)),
        out_shape=jax.ShapeDtypeStruct((s, n_ch), BF16),
        scratch_shapes=[pltpu.VMEM((nc, ts + 2 * CONV_HALO, LANES), F32),
                        pltpu.VMEM((nc, ts, LANES), F32)],
        compiler_params=_cparams("parallel"),
        name="conv_group",
    )(z, z, z, z, z, z, w, chunked(dw_b), chunked(ln_w), chunked(ln_b), chunked(on_w))


def _q_kernel(z_ref, nw_ref, w_ref, cos_ref, sin_ref, hw_ref, q_ref, zn_ref, *, q_scale):
    @pl.when(pl.program_id(1) == 0)
    def _():
        z = z_ref[...]
        zn_ref[...] = (z * _rms_scale(z, z.shape[-1]) * nw_ref[...]).astype(BF16)

    hw = hw_ref[...]
    for hh in range(w_ref.shape[0]):
        r = jnp.dot(zn_ref[...], w_ref[hh], preferred_element_type=F32)
        nope = r[:, :LANES]
        rope = r[:, LANES:2 * LANES] * cos_ref[...] + r[:, 2 * LANES:] * sin_ref[...]
        ss = jnp.sum(nope * nope, axis=-1, keepdims=True) + jnp.sum(rope * rope, axis=-1, keepdims=True)
        inv = lax.rsqrt(ss * (1.0 / QK_HEAD_DIM) + EPS) * q_scale
        q_ref[hh, :LANES, :] = (nope * inv * hw[:, :LANES]).T.astype(BF16)
        q_ref[hh, LANES:, :] = (rope * inv * hw[:, LANES:]).T.astype(BF16)


def _q_proj(z, nw, w3, cos_t, sin_t, hw, *, col_blk, tm, hb, q_scale):
    s = z.shape[0]
    rank = w3.shape[1]
    kern = functools.partial(_q_kernel, q_scale=q_scale)
    return pl.pallas_call(
        kern,
        grid=(s // tm, N_HEADS // hb),
        in_specs=[pl.BlockSpec((tm, rank), lambda i, h: (i, col_blk)),
                  pl.BlockSpec((1, rank), lambda i, h: (0, 0)),
                  pl.BlockSpec((hb, rank, 3 * LANES), lambda i, h: (h, 0, 0)),
                  pl.BlockSpec((tm, LANES), lambda i, h: (i, 0)),
                  pl.BlockSpec((tm, LANES), lambda i, h: (i, 0)),
                  pl.BlockSpec((1, QK_PAD_DIM), lambda i, h: (0, 0))],
        out_specs=pl.BlockSpec((hb, QK_PAD_DIM, tm), lambda i, h: (h, 0, i)),
        out_shape=jax.ShapeDtypeStruct((N_HEADS, QK_PAD_DIM, s), BF16),
        scratch_shapes=[pltpu.VMEM((tm, rank), BF16)],
        compiler_params=_cparams("parallel", "arbitrary"),
        name="q_proj",
    )(z, nw, w3, cos_t, sin_t, hw)


def _kv_kernel(z_ref, kpe_ref, kpes_ref, nw_ref, w_ref, cos_ref, sin_ref, hw_ref, k_ref, v_ref,
               zn_ref, kr_ref, ss_ref):
    @pl.when(pl.program_id(1) == 0)
    def _():
        z = z_ref[...]
        zn_ref[...] = (z * _rms_scale(z, z.shape[-1]) * nw_ref[...]).astype(BF16)
        kr = kpe_ref[...] * cos_ref[...] + kpes_ref[...] * sin_ref[...]
        kr_ref[...] = kr
        ss_ref[...] = jnp.sum(kr * kr, axis=-1, keepdims=True)

    hw = hw_ref[...]
    for hh in range(w_ref.shape[0]):
        r = jnp.dot(zn_ref[...], w_ref[hh], preferred_element_type=F32)
        kn = r[:, :LANES]
        ss = jnp.sum(kn * kn, axis=-1, keepdims=True) + ss_ref[...]
        inv = lax.rsqrt(ss * (1.0 / QK_HEAD_DIM) + EPS)
        k_ref[hh, :, :LANES] = (kn * inv * hw[:, :LANES]).astype(BF16)
        k_ref[hh, :, LANES:] = (kr_ref[...] * inv * hw[:, LANES:]).astype(BF16)
        v_ref[hh, :V_HEAD_DIM, :] = r[:, LANES:].T.astype(BF16)
        ones_row = lax.broadcasted_iota(I32, (V_AUG_DIM - V_HEAD_DIM, r.shape[0]), 0) == 0
        v_ref[hh, V_HEAD_DIM:, :] = jnp.where(ones_row, 1.0, 0.0).astype(BF16)


def _kv_proj(z, nw, w3, cos_t, sin_t, hw, *, ckv_blk, kpe_blk, tm, hb):
    s = z.shape[0]
    rank = w3.shape[1]
    return pl.pallas_call(
        _kv_kernel,
        grid=(s // tm, N_HEADS // hb),
        in_specs=[pl.BlockSpec((tm, rank), lambda i, h: (i, ckv_blk)),
                  pl.BlockSpec((tm, LANES), lambda i, h: (i, kpe_blk)),
                  pl.BlockSpec((tm, LANES), lambda i, h: (i, kpe_blk + 1)),
                  pl.BlockSpec((1, rank), lambda i, h: (0, 0)),
                  pl.BlockSpec((hb, rank, 2 * LANES), lambda i, h: (h, 0, 0)),
                  pl.BlockSpec((tm, LANES), lambda i, h: (i, 0)),
                  pl.BlockSpec((tm, LANES), lambda i, h: (i, 0)),
                  pl.BlockSpec((1, QK_PAD_DIM), lambda i, h: (0, 0))],
        out_specs=[pl.BlockSpec((hb, tm, QK_PAD_DIM), lambda i, h: (h, i, 0)),
                   pl.BlockSpec((hb, V_AUG_DIM, tm), lambda i, h: (h, 0, i))],
        out_shape=[jax.ShapeDtypeStruct((N_HEADS, s, QK_PAD_DIM), BF16),
                   jax.ShapeDtypeStruct((N_HEADS, V_AUG_DIM, s), BF16)],
        scratch_shapes=[pltpu.VMEM((tm, rank), BF16),
                        pltpu.VMEM((tm, LANES), F32),
                        pltpu.VMEM((tm, 1), F32)],
        compiler_params=_cparams("parallel", "arbitrary"),
        name="kv_proj",
    )(z, z, z, nw, w3, cos_t, sin_t, hw)


def _attn_kernel(qt_ref, k_ref, vt_ref, o_ref, st_ref, *, tk):
    qt = qt_ref[0]
    tq = qt.shape[1]
    nk = k_ref.shape[1] // tk

    def scores(j, slot):
        start = pl.multiple_of(j * tk, tk)
        st_ref[slot] = jnp.dot(k_ref[0, pl.ds(start, tk), :], qt, preferred_element_type=F32)

    def update(j, slot, carry):
        m, acc = carry
        start = pl.multiple_of(j * tk, tk)
        vts = vt_ref[0, :, pl.ds(start, tk)]
        st = st_ref[slot]
        m_new = jnp.maximum(m, jnp.max(st, axis=0, keepdims=True))
        p = jnp.exp2(st - m_new)
        alpha = jnp.exp2(m - m_new)
        acc = alpha * acc + jnp.dot(vts, p.astype(BF16), preferred_element_type=F32)
        return m_new, acc

    def body(jj, carry):
        j = 2 * jj
        scores(j + 1, 1)
        carry = update(j, 0, carry)
        scores(jnp.minimum(j + 2, nk - 1), 0)
        return update(j + 1, 1, carry)

    m0 = jnp.full((1, tq), -jnp.inf, F32)
    acc0 = jnp.zeros((V_AUG_DIM, tq), F32)
    scores(0, 0)
    _, acc = lax.fori_loop(0, nk // 2, body, (m0, acc0), unroll=4)
    o_ref[...] = (acc[:V_HEAD_DIM] / acc[V_HEAD_DIM:V_HEAD_DIM + 1]).T


def _attention(qt, k, vt, *, tq, tk):
    h, s, _ = k.shape
    kern = functools.partial(_attn_kernel, tk=tk)
    return pl.pallas_call(
        kern,
        grid=(h, s // tq),
        in_specs=[pl.BlockSpec((1, QK_PAD_DIM, tq), lambda hh, i: (hh, 0, i)),
                  pl.BlockSpec((1, s, QK_PAD_DIM), lambda hh, i: (hh, 0, 0)),
                  pl.BlockSpec((1, V_AUG_DIM, s), lambda hh, i: (hh, 0, 0))],
        out_specs=pl.BlockSpec((tq, V_HEAD_DIM), lambda hh, i: (i, hh)),
        out_shape=jax.ShapeDtypeStruct((s, h * V_HEAD_DIM), F32),
        scratch_shapes=[pltpu.VMEM((2, tk, tq), F32)],
        compiler_params=_cparams("parallel", "arbitrary"),
        name="attention",
    )(qt, k, vt)


def _out_proj_kernel(a_ref, c_ref, x_ref, nw_ref, wa_ref, wc_ref, o_ref, an_ref):
    @pl.when(pl.program_id(1) == 0)
    def _():
        a = a_ref[...]
        an_ref[...] = (a * _rms_scale(a, a.shape[-1]) * nw_ref[...]).astype(BF16)

    acc = jnp.dot(an_ref[...], wa_ref[...], preferred_element_type=F32)
    acc = acc + jnp.dot(c_ref[...], wc_ref[...], preferred_element_type=F32)
    o_ref[...] = x_ref[...] + acc


def _out_proj(attn, conv_n, x, nw, w_o, *, tm, tn):
    s, da = attn.shape
    dc = conv_n.shape[1]
    d = x.shape[1]
    assert da == dc, "the two head groups share one W_o row-block size"
    return pl.pallas_call(
        _out_proj_kernel,
        grid=(s // tm, d // tn),
        in_specs=[pl.BlockSpec((tm, da), lambda i, j: (i, 0)),
                  pl.BlockSpec((tm, dc), lambda i, j: (i, 0)),
                  pl.BlockSpec((tm, tn), lambda i, j: (i, j)),
                  pl.BlockSpec((1, da), lambda i, j: (0, 0)),
                  pl.BlockSpec((da, tn), lambda i, j: (0, j)),
                  pl.BlockSpec((dc, tn), lambda i, j: (1, j))],
        out_specs=pl.BlockSpec((tm, tn), lambda i, j: (i, j)),
        out_shape=jax.ShapeDtypeStruct((s, d), F32),
        scratch_shapes=[pltpu.VMEM((tm, da), BF16)],
        compiler_params=_cparams("parallel", "arbitrary"),
        name="out_proj",
    )(attn, conv_n, x, nw, w_o, w_o)


def _router_kernel(x_ref, nw_ref, wr_ref, br_ref, tri_ref, h_ref, idx_ref, gate_ref, rank_ref, cnt_ref, run_ref):
    @pl.when(pl.program_id(0) == 0)
    def _():
        run_ref[...] = jnp.zeros_like(run_ref)

    x = x_ref[...]
    h = x * _rms_scale(x, x.shape[-1]) * nw_ref[...]
    half = h.shape[1] // 2
    lo = lax.bitcast_convert_type(h[:, :half].astype(BF16).astype(F32), U32)
    hi = lax.bitcast_convert_type(h[:, half:].astype(BF16).astype(F32), U32)
    h_ref[...] = (lo >> 16) | hi
    tm = x.shape[0]
    logits = lax.dot_general(wr_ref[...], h, (((1,), (1,)), ((), ())), preferred_element_type=F32,
                             precision=lax.Precision.HIGHEST) + br_ref[...]
    eio = lax.broadcasted_iota(I32, (N_EXPERTS, tm), 0)
    vals, idxs = [], []
    cur = logits
    for _ in range(TOP_K):
        m = jnp.max(cur, axis=0, keepdims=True)
        idx = jnp.min(jnp.where(cur == m, eio, N_EXPERTS), axis=0, keepdims=True)
        vals.append(m)
        idxs.append(idx)
        cur = jnp.where(eio == idx, -jnp.inf, cur)
    es = [jnp.exp(v - vals[0]) for v in vals]
    den = es[0] + es[1] + es[2] + es[3]
    base = run_ref[...]
    for k in range(TOP_K):
        gate_ref[k:k + 1, :] = es[k] / den
        idx_ref[k:k + 1, :] = idxs[k]
        onehot = eio == idxs[k]
        before = jnp.dot(onehot.astype(BF16), tri_ref[...], preferred_element_type=F32)
        rank = jnp.sum(jnp.where(onehot, before + base, 0.0), axis=0, keepdims=True)
        rank_ref[k:k + 1, :] = rank.astype(I32)
        base = base + jnp.sum(onehot.astype(F32), axis=1, keepdims=True)
    run_ref[...] = base
    cnt_ref[...] = base.astype(I32)


def _router(x1, nw, wr_t, br, *, tm):
    t, d = x1.shape
    tri = (lax.broadcasted_iota(I32, (tm, tm), 0) < lax.broadcasted_iota(I32, (tm, tm), 1)).astype(BF16)
    row4 = pl.BlockSpec((TOP_K, tm), lambda i: (0, i))
    return pl.pallas_call(
        _router_kernel,
        grid=(t // tm,),
        in_specs=[pl.BlockSpec((tm, d), lambda i: (i, 0)),
                  pl.BlockSpec((1, d), lambda i: (0, 0)),
                  pl.BlockSpec((N_EXPERTS, d), lambda i: (0, 0)),
                  pl.BlockSpec((N_EXPERTS, 1), lambda i: (0, 0)),
                  pl.BlockSpec((tm, tm), lambda i: (0, 0))],
        out_specs=[pl.BlockSpec((tm, d // 2), lambda i: (i, 0)), row4, row4, row4,
                   pl.BlockSpec((N_EXPERTS, 1), lambda i: (0, 0))],
        out_shape=[jax.ShapeDtypeStruct((t, d // 2), U32),
                   jax.ShapeDtypeStruct((TOP_K, t), I32),
                   jax.ShapeDtypeStruct((TOP_K, t), F32),
                   jax.ShapeDtypeStruct((TOP_K, t), I32),
                   jax.ShapeDtypeStruct((N_EXPERTS, 1), I32)],
        scratch_shapes=[pltpu.VMEM((N_EXPERTS, 1), F32)],
        compiler_params=_cparams("arbitrary"),
        name="router",
    )(x1, nw, wr_t, br, tri)


def _gather_kernel(zflag_ref, dest_ref, h_ref, xs_hbm, zero_ref, sem, zsem, *, tt, tm):
    n_tiles = xs_hbm.shape[0] // tm

    @pl.when(pl.program_id(0) == 0)
    def _():
        zero_ref[...] = jnp.zeros_like(zero_ref)

        def zero_copy(g):
            return pltpu.make_async_copy(zero_ref, xs_hbm.at[pl.ds(pl.multiple_of(g * tm, tm), tm)], zsem)

        def zstart(g, carry):
            @pl.when(zflag_ref[g] != 0)
            def _():
                zero_copy(g).start()
            return carry

        def zwait(g, carry):
            @pl.when(zflag_ref[g] != 0)
            def _():
                zero_copy(g).wait()
            return carry

        lax.fori_loop(0, n_tiles, zstart, 0)
        lax.fori_loop(0, n_tiles, zwait, 0)

    def row_copy(src_row, dst_row):
        return pltpu.make_async_copy(h_ref.at[pl.ds(src_row, 1)], xs_hbm.at[pl.ds(dst_row, 1)], sem)

    def start_body(t, carry):
        for k in range(TOP_K):
            row_copy(t, dest_ref[0, k, t]).start(priority=k % 2)
        return carry

    def wait_body(t, carry):
        for k in range(TOP_K):
            row_copy(0, 0).wait()
        return carry

    lax.fori_loop(0, tt, start_body, 0)
    lax.fori_loop(0, tt, wait_body, 0)


def _moe_gather(zflag, h2, dest, n_rows, *, tt, tm):
    t, d = h2.shape
    dest3 = dest.reshape(TOP_K, t // tt, tt).transpose(1, 0, 2)
    kern = functools.partial(_gather_kernel, tt=tt, tm=tm)
    return pl.pallas_call(
        kern,
        grid_spec=pltpu.PrefetchScalarGridSpec(
            num_scalar_prefetch=1,
            grid=(t // tt,),
            in_specs=[pl.BlockSpec((1, TOP_K, tt), lambda i, zf: (i, 0, 0), memory_space=pltpu.SMEM),
                      pl.BlockSpec((tt, d), lambda i, zf: (i, 0))],
            out_specs=pl.BlockSpec(memory_space=pl.ANY),
            scratch_shapes=[pltpu.VMEM((tm, d), h2.dtype), pltpu.SemaphoreType.DMA(()),
                            pltpu.SemaphoreType.DMA(())],
        ),
        out_shape=jax.ShapeDtypeStruct((n_rows, d), h2.dtype),
        compiler_params=_cparams("arbitrary"),
        name="moe_gather",
    )(zflag, dest3, h2)


def _expert_row_tiles(ts_ref, nt_ref, nv_ref, cnt_ref, src_hbm, dst_hbm, in_buf, out_buf, zero_ref, in_sem, out_sem, zsem,
                      compute):
    j, e = pl.program_id(0), pl.program_id(1)
    nj, ne = pl.num_programs(0), pl.num_programs(1)
    tm, tn = out_buf.shape[1], out_buf.shape[2]
    n_tiles = dst_hbm.shape[0] // tm
    nv = nv_ref[0]
    t0 = ts_ref[e]
    col = pl.multiple_of(j * tn, tn)

    def rows(g):
        return pl.ds(pl.multiple_of(g * tm, tm), tm)

    def in_copy(g, slot):
        return pltpu.make_async_copy(src_hbm.at[rows(g)], in_buf.at[slot], in_sem.at[slot])

    def out_copy(g, slot):
        return pltpu.make_async_copy(out_buf.at[slot], dst_hbm.at[rows(g), pl.ds(col, tn)], out_sem.at[slot])

    def zero_copy(g):
        return pltpu.make_async_copy(zero_ref, dst_hbm.at[rows(g), pl.ds(col, tn)], zsem)

    @pl.when(jnp.logical_and(j == 0, e == 0))
    def _():
        zero_ref[...] = jnp.zeros_like(zero_ref)

    def tile_body(t, carry):
        g = t0 + t
        seq = j * nv + g
        slot = seq % 2

        @pl.when(seq == 0)
        def _():
            in_copy(g, slot).start()

        in_copy(g, slot).wait()
        last_of_sweep = g + 1 == nv

        @pl.when(jnp.logical_or(jnp.logical_not(last_of_sweep), j + 1 < nj))
        def _():
            in_copy(jnp.where(last_of_sweep, 0, g + 1), 1 - slot).start()

        @pl.when(seq >= 2)
        def _():
            out_copy(g, slot).wait()

        half_rows = tm // 2
        few_rows = cnt_ref[e] - t * tm <= half_rows

        @pl.when(few_rows)
        def _():
            out_buf[slot, :half_rows] = compute(in_buf[slot, :half_rows])
            out_buf[slot, half_rows:] = jnp.zeros((tm - half_rows, tn), out_buf.dtype)

        @pl.when(jnp.logical_not(few_rows))
        def _():
            out_buf[slot] = compute(in_buf[slot])

        out_copy(g, slot).start()
        return carry

    lax.fori_loop(0, nt_ref[e], tile_body, 0)

    @pl.when(e == ne - 1)
    def _():
        def zstart(g, carry):
            zero_copy(g).start()
            return carry

        def zwait(g, carry):
            zero_copy(g).wait()
            return carry

        lax.fori_loop(nv, n_tiles, zstart, 0)
        lax.fori_loop(nv, n_tiles, zwait, 0)

    @pl.when(jnp.logical_and(e == ne - 1, j == nj - 1))
    def _():
        total = nj * nv

        @pl.when(total >= 1)
        def _():
            out_copy(0, (total - 1) % 2).wait()

        @pl.when(total >= 2)
        def _():
            out_copy(0, total % 2).wait()


def _expert_up_kernel(ts_ref, nt_ref, nv_ref, cnt_ref, wg_ref, bg_ref, wu_ref, bu_ref, xs_hbm, act_hbm,
                      wgc_ref, wuc_ref, in_buf, out_buf, zero_ref, in_sem, out_sem, zsem):
    @pl.when(nt_ref[pl.program_id(1)] > 0)
    def _():
        wgc_ref[...] = wg_ref[0].astype(BF16)
        wuc_ref[...] = wu_ref[0].astype(BF16)

    def compute(xw):
        half = xw.shape[1]
        x_lo = lax.bitcast_convert_type(xw << 16, F32).astype(BF16)
        x_hi = lax.bitcast_convert_type((xw >> 16) << 16, F32).astype(BF16)

        def proj(w_ref, b_ref):
            y = jnp.dot(x_lo, w_ref[:half, :], preferred_element_type=F32)
            return y + jnp.dot(x_hi, w_ref[half:, :], preferred_element_type=F32) + b_ref[0]

        gate = proj(wgc_ref, bg_ref)
        lin = proj(wuc_ref, bu_ref)
        gate = jnp.minimum(gate, SWIGLU_LIMIT)
        lin = jnp.clip(lin, -SWIGLU_LIMIT, SWIGLU_LIMIT)
        return (gate * jax.nn.sigmoid(SWIGLU_ALPHA * gate) * (lin + 1.0)).astype(BF16)

    _expert_row_tiles(ts_ref, nt_ref, nv_ref, cnt_ref, xs_hbm, act_hbm, in_buf, out_buf, zero_ref, in_sem, out_sem, zsem,
                      compute)


def _w_blk(j, e, ts, nt, nv, cnt):
    return (e, 0, j)


def _row_tile_scratch(tm, k_in, in_dtype, tn, out_dtype):
    return [pltpu.VMEM((2, tm, k_in), in_dtype), pltpu.VMEM((2, tm, tn), out_dtype), pltpu.VMEM((tm, tn), out_dtype),
            pltpu.SemaphoreType.DMA((2,)), pltpu.SemaphoreType.DMA((2,)), pltpu.SemaphoreType.DMA(())]


def _expert_up(plan, xs, wg, bg, wu, bu, *, tm, tn):
    r, dw = xs.shape
    n_e, d, de = wg.shape
    assert 2 * dw == d, "xs rows hold two bf16 columns per 32-bit word"
    return pl.pallas_call(
        _expert_up_kernel,
        grid_spec=pltpu.PrefetchScalarGridSpec(
            num_scalar_prefetch=4,
            grid=(de // tn, n_e),
            in_specs=[pl.BlockSpec((1, d, tn), _w_blk),
                      pl.BlockSpec((1, 1, tn), _w_blk),
                      pl.BlockSpec((1, d, tn), _w_blk),
                      pl.BlockSpec((1, 1, tn), _w_blk),
                      pl.BlockSpec(memory_space=pl.ANY)],
            out_specs=pl.BlockSpec(memory_space=pl.ANY),
            scratch_shapes=[pltpu.VMEM((d, tn), BF16), pltpu.VMEM((d, tn), BF16)]
            + _row_tile_scratch(tm, dw, xs.dtype, tn, BF16),
        ),
        out_shape=jax.ShapeDtypeStruct((r, de), BF16),
        compiler_params=_cparams("arbitrary", "arbitrary"),
        name="expert_up",
    )(*plan, wg, bg, wu, bu, xs)


def _expert_down_kernel(ts_ref, nt_ref, nv_ref, cnt_ref, w_ref, b_ref, act_hbm, ys_hbm,
                        wc_ref, in_buf, out_buf, zero_ref, in_sem, out_sem, zsem):
    @pl.when(nt_ref[pl.program_id(1)] > 0)
    def _():
        wc_ref[...] = w_ref[0].astype(BF16)

    def compute(a):
        return jnp.dot(a, wc_ref[...], preferred_element_type=F32) + b_ref[0]

    _expert_row_tiles(ts_ref, nt_ref, nv_ref, cnt_ref, act_hbm, ys_hbm, in_buf, out_buf, zero_ref, in_sem, out_sem, zsem,
                      compute)


def _expert_down(plan, act, wd, bd, *, tm, tn):
    r, de = act.shape
    n_e, _, d = wd.shape
    return pl.pallas_call(
        _expert_down_kernel,
        grid_spec=pltpu.PrefetchScalarGridSpec(
            num_scalar_prefetch=4,
            grid=(d // tn, n_e),
            in_specs=[pl.BlockSpec((1, de, tn), _w_blk),
                      pl.BlockSpec((1, 1, tn), _w_blk),
                      pl.BlockSpec(memory_space=pl.ANY)],
            out_specs=pl.BlockSpec(memory_space=pl.ANY),
            scratch_shapes=[pltpu.VMEM((de, tn), BF16)] + _row_tile_scratch(tm, de, act.dtype, tn, F32),
        ),
        out_shape=jax.ShapeDtypeStruct((r, d), F32),
        compiler_params=_cparams("arbitrary", "arbitrary"),
        name="expert_down",
    )(*plan, wd, bd, act)


def _combine_kernel(dest_ref, dest_next_ref, x_ref, g_ref, ys_hbm, o_ref, buf_ref, sem, *, tt):
    i = pl.program_id(0)
    slot = i % 2

    def row_copy(src_row, s, k, t):
        return pltpu.make_async_copy(ys_hbm.at[pl.ds(src_row, 1)], buf_ref.at[s, k, pl.ds(t, 1)], sem.at[s])

    def fetch(d_ref, s):
        def start_body(t, carry):
            for k in range(TOP_K):
                row_copy(d_ref[0, k, t], s, k, t).start(priority=k % 2)
            return carry

        lax.fori_loop(0, tt, start_body, 0)

    @pl.when(i == 0)
    def _():
        fetch(dest_ref, slot)

    @pl.when(i + 1 < pl.num_programs(0))
    def _():
        fetch(dest_next_ref, 1 - slot)

    def wait_body(t, carry):
        for k in range(TOP_K):
            row_copy(0, slot, k, t).wait()
        return carry

    lax.fori_loop(0, tt, wait_body, 0)
    g = g_ref[...]
    acc = x_ref[...]
    for k in range(TOP_K):
        acc = acc + g[:, k:k + 1] * buf_ref[slot, k]
    o_ref[...] = acc


def _moe_combine(dest, x1, gates_t, ys, *, tt):
    t, d = x1.shape
    n_steps = t // tt
    dest3 = dest.reshape(TOP_K, n_steps, tt).transpose(1, 0, 2)
    kern = functools.partial(_combine_kernel, tt=tt)
    return pl.pallas_call(
        kern,
        grid=(n_steps,),
        in_specs=[pl.BlockSpec((1, TOP_K, tt), lambda i: (i, 0, 0), memory_space=pltpu.SMEM),
                  pl.BlockSpec((1, TOP_K, tt), lambda i: (jnp.minimum(i + 1, n_steps - 1), 0, 0),
                               memory_space=pltpu.SMEM),
                  pl.BlockSpec((tt, d), lambda i: (i, 0)),
                  pl.BlockSpec((tt, TOP_K), lambda i: (i, 0)),
                  pl.BlockSpec(memory_space=pl.ANY)],
        out_specs=pl.BlockSpec((tt, d), lambda i: (i, 0)),
        out_shape=jax.ShapeDtypeStruct((t, d), F32),
        scratch_shapes=[pltpu.VMEM((2, TOP_K, tt, d), F32), pltpu.SemaphoreType.DMA((2,))],
        compiler_params=_cparams("arbitrary"),
        name="moe_combine",
    )(dest3, dest3, x1, gates_t, ys)


def _rope_tables(positions):
    half = QK_ROPE_DIM // 2
    inv_freq = ROPE_THETA ** (-jnp.arange(0, QK_ROPE_DIM, 2, dtype=F32) / QK_ROPE_DIM)
    ang = positions.astype(F32)[:, None] * inv_freq
    cos, sin = jnp.cos(ang), jnp.sin(ang)
    zeros = jnp.zeros((positions.shape[0], LANES - 2 * half), F32)
    return jnp.concatenate([cos, cos, zeros], axis=1), jnp.concatenate([-sin, sin, zeros], axis=1)


def _swap_halves(w):
    half = w.shape[-1] // 2
    return jnp.concatenate([w[..., half:], w[..., :half]], axis=-1)


def _layer(x, positions, attn_norm_w, w_in, q_a_norm_w, w_uq, kv_a_norm_w, w_ukv, q_head_norm_w, k_head_norm_w,
           conv_dw_w, conv_dw_b, conv_ln_w, conv_ln_b, attn_out_norm_w, conv_out_norm_w, w_o, ffn_norm_w,
           w_router, b_router, w_gate, b_gate, w_up, b_up, w_down, b_down):
    s, d = x.shape
    q_rank = q_a_norm_w.shape[0]
    kv_rank = kv_a_norm_w.shape[0]
    n_conv = conv_dw_b.shape[0]
    glu_w = 2 * n_conv
    de = w_gate.shape[2]

    pad64 = jnp.zeros((d, LANES - QK_ROPE_DIM), F32)
    w_kpe = w_in[:, glu_w + q_rank + kv_rank:]
    z_cols = glu_w + q_rank + kv_rank + 2 * LANES
    in_tn = 1024
    z_pad = -z_cols % in_tn
    w_in_p = jnp.concatenate(
        [w_in[:, :glu_w + q_rank + kv_rank], w_kpe, pad64, _swap_halves(w_kpe), pad64, jnp.zeros((d, z_pad), F32)],
        axis=1).astype(BF16)

    wq = w_uq.reshape(q_rank, N_HEADS, QK_HEAD_DIM)
    wq_pe = wq[:, :, QK_NOPE_DIM:]
    zq = jnp.zeros((q_rank, N_HEADS, LANES - QK_ROPE_DIM), F32)
    wq3 = jnp.concatenate([wq[:, :, :QK_NOPE_DIM], wq_pe, zq, _swap_halves(wq_pe), zq], axis=2)
    wq3 = wq3.transpose(1, 0, 2).astype(BF16)
    wkv3 = w_ukv.reshape(kv_rank, N_HEADS, QK_NOPE_DIM + V_HEAD_DIM).transpose(1, 0, 2).astype(BF16)
    zero_hw = jnp.zeros((QK_PAD_DIM - QK_HEAD_DIM,), F32)
    q_hw = jnp.concatenate([q_head_norm_w, zero_hw])[None]
    k_hw = jnp.concatenate([k_head_norm_w, zero_hw])[None]
    cos_t, sin_t = _rope_tables(positions)

    z = _norm_matmul(x, attn_norm_w[None], w_in_p, tm=512, tn=in_tn)
    conv_n = _conv_group(z, conv_dw_w, conv_dw_b, conv_ln_w, conv_ln_b, conv_out_norm_w, n_ch=n_conv, ts=256)
    q_scale = QK_HEAD_DIM ** -0.5 * LOG2E
    q = _q_proj(z, q_a_norm_w[None], wq3, cos_t, sin_t, q_hw, col_blk=glu_w // q_rank, tm=512, hb=8,
                q_scale=q_scale)
    k, v = _kv_proj(z, kv_a_norm_w[None], wkv3, cos_t, sin_t, k_hw, ckv_blk=(glu_w + q_rank) // kv_rank,
                    kpe_blk=(glu_w + q_rank + kv_rank) // LANES, tm=512, hb=8)
    attn = _attention(q, k, v, tq=512, tk=512)
    x1 = _out_proj(attn, conv_n, x, attn_out_norm_w[None], w_o.astype(BF16), tm=512, tn=1024)

    tm_e = 256
    h2, top_idx, gates, rank, counts = _router(x1, ffn_norm_w[None], w_router.T, b_router[:, None], tm=512)
    counts = counts[:, 0]
    tiles_per_e = (counts + tm_e - 1) // tm_e
    tile_end = jnp.cumsum(tiles_per_e)
    tile_start = tile_end - tiles_per_e
    n_valid = tile_end[-1:]
    expert_ids = jnp.arange(N_EXPERTS, dtype=I32)[:, None, None]
    row_start = (tile_start * tm_e)[:, None, None]
    dest = jnp.sum(jnp.where(top_idx[None] == expert_ids, row_start, 0), axis=0) + rank
    n_rows = s * TOP_K + N_EXPERTS * tm_e
    tile_ids = jnp.arange(n_rows // tm_e, dtype=I32)
    has_pad_rows = jnp.any((tile_ids[:, None] == tile_end[None, :] - 1) & (counts % tm_e != 0)[None, :], axis=1)
    zflag = (has_pad_rows | (tile_ids >= n_valid[0])).astype(I32)

    xs = _moe_gather(zflag, h2, dest, n_rows, tt=512, tm=tm_e)
    plan = (tile_start, tiles_per_e, n_valid, counts)
    act = _expert_up(plan, xs, w_gate, b_gate[:, None, :], w_up, b_up[:, None, :], tm=tm_e, tn=512)
    ys = _expert_down(plan, act, w_down, b_down[:, None, :], tm=tm_e, tn=2048)
    return _moe_combine(dest, x1, gates.T, ys, tt=128)


def kernel(x, positions, attn_norm_w, w_in, q_a_norm_w, w_uq, kv_a_norm_w, w_ukv, q_head_norm_w, k_head_norm_w,
           conv_dw_w, conv_dw_b, conv_ln_w, conv_ln_b, attn_out_norm_w, conv_out_norm_w, w_o, ffn_norm_w,
           w_router, b_router, w_gate, b_gate, w_up, b_up, w_down, b_down):
    b, s, d = x.shape
    depth = attn_norm_w.shape[0]
    assert b == 1, "one sequence per call"
    xs = x.reshape(s, d)
    for l in range(depth):
        xs = _layer(xs, positions[0], attn_norm_w[l], w_in[l], q_a_norm_w[l], w_uq[l], kv_a_norm_w[l], w_ukv[l],
                    q_head_norm_w[l], k_head_norm_w[l], conv_dw_w[l][:, 0, :], conv_dw_b[l], conv_ln_w[l],
                    conv_ln_b[l], attn_out_norm_w[l], conv_out_norm_w[l], w_o[l], ffn_norm_w[l], w_router[l],
                    b_router[l], w_gate[l], b_gate[l], w_up[l], b_up[l], w_down[l], b_down[l])
    return xs.reshape(b, s, d)
```
